```python
import math
import jax, jax.numpy as jnp
from jax import lax
import numpy as np

D_MODEL = 2048
BATCH = 4
SEQ = 2048
DEPTH = 2
DEC_BATCH = 128
DEC_SEQ = 8
PAST_LEN = 16384
PAGE_SIZE = 128

N_BRANCH = 3
BR_W = 1024
HG_HEADS = 8
HG_DK = 128
HG_DV = 128
F_MIN = 1e-30
RW_HEADS = 16
RW_N = 64
RW_DECAY_LORA = 96
RW_AAA_LORA = 96
RW_MV_LORA = 64
RW_SHIFT_W = 3 * BR_W + RW_DECAY_LORA + RW_AAA_LORA
RW_GN_EPS = 64e-5
RN_HEADS = 4
RN_DK = 256
RN_DV = 256
ROPE_BASE = 10000.0
CHUNK = 64
PLE_DIM = 256
EPS = 1e-6
IN_SIZES = (3 * BR_W, RW_SHIFT_W, 3 * BR_W, N_BRANCH * BR_W, N_BRANCH * D_MODEL)
D_IN = 3 * BR_W + RW_SHIFT_W + 3 * BR_W + N_BRANCH * BR_W + N_BRANCH * D_MODEL

kernel_name = 'hgrn2_rwkv7_retnet_parallel_decode_step'


def rmsnorm(x, g=None):
    xf = x.astype(jnp.float32)
    y = xf * lax.rsqrt(jnp.mean(xf * xf, axis=-1, keepdims=True) + EPS)
    if g is not None:
        y = y * g.astype(jnp.float32)
    return y.astype(x.dtype)


def _split(z, sizes):
    out, o = [], 0
    for s in sizes:
        out.append(z[..., o:o + s])
        o += s
    return out


def _heads(t, h):
    return t.reshape(t.shape[:-1] + (h, t.shape[-1] // h))


def _to_chunks(t, c):
    b, tt, h, x = t.shape
    return t.reshape(b, tt // c, c, h, x).transpose(1, 0, 3, 2, 4)


def _from_chunks(o):
    n, b, h, c, x = o.shape
    return o.transpose(1, 0, 3, 2, 4).reshape(b, n * c, h, x)


def rotary(x, pos):
    d = x.shape[-1]
    inv = ROPE_BASE ** (-jnp.arange(0, d, 2, dtype=jnp.float32) / d)
    ang = pos.astype(jnp.float32)[:, None] * inv[None, :]
    cos = jnp.cos(ang)[None, :, None, :]
    sin = jnp.sin(ang)[None, :, None, :]
    x1, x2 = x[..., : d // 2], x[..., d // 2:]
    return jnp.concatenate([x1 * cos - x2 * sin, x1 * sin + x2 * cos], axis=-1)


def token_shift(u, prev, mu):
    u_prev = jnp.concatenate([prev[:, None, :], u[:, :-1]], axis=1)
    return u + (u_prev - u) * mu, u[:, -1]


def gla_chunked(q, k, v, logf, s0):
    c = math.gcd(q.shape[1], CHUNK)
    tri = jnp.tril(jnp.ones((c, c), dtype=bool))[:, :, None]

    def step(s, blk):
        qc, kc, vc, gc = blk
        b = jnp.cumsum(gc, axis=2)
        diff = b[:, :, :, None, :] - b[:, :, None, :, :]
        dec = jnp.where(tri, jnp.exp(jnp.where(tri, diff, 0.0)), 0.0)
        att = jnp.einsum('bhtk,bhsk,bhtsk->bhts', qc, kc, dec)
        o = jnp.einsum('bhts,bhsv->bhtv', att, vc) + jnp.einsum('bhtk,bhkv->bhtv', qc * jnp.exp(b), s)
        bl = b[:, :, -1:]
        s = jnp.exp(bl[:, :, 0])[..., None] * s + jnp.einsum('bhsk,bhsv->bhkv', kc * jnp.exp(bl - b), vc)
        return s, o

    s, o = lax.scan(step, s0, (_to_chunks(q, c), _to_chunks(k, c), _to_chunks(v, c), _to_chunks(logf, c)))
    return _from_chunks(o), s


def retention_chunked(q, k, v, log_gamma, s0):
    c = math.gcd(q.shape[1], CHUNK)
    j = jnp.arange(c, dtype=jnp.float32)
    rel = j[:, None] - j[None, :]
    dmat = jnp.where(rel >= 0, jnp.exp(log_gamma[:, None, None] * jnp.maximum(rel, 0.0)), 0.0)
    q_dec = jnp.exp(log_gamma[:, None] * (j + 1.0))[..., None]
    k_dec = jnp.exp(log_gamma[:, None] * (c - 1.0 - j))[..., None]
    c_dec = jnp.exp(log_gamma * c)[:, None, None]

    def step(s, blk):
        qc, kc, vc = blk
        att = jnp.einsum('bhtk,bhsk->bhts', qc, kc) * dmat
        o = jnp.einsum('bhts,bhsv->bhtv', att, vc) + jnp.einsum('bhtk,bhkv->bhtv', qc, s) * q_dec
        s = c_dec * s + jnp.einsum('bhsk,bhsv->bhkv', kc * k_dec, vc)
        return s, o

    s, o = lax.scan(step, s0, (_to_chunks(q, c), _to_chunks(k, c), _to_chunks(v, c)))
    return _from_chunks(o), s


def rwkv7_scan(r, logw, k, v, a, b, s0):
    def step(s, inp):
        rt, wt, kt, vt, at, bt = inp
        sa = jnp.einsum('bhij,bhj->bhi', s, at)
        s = s * jnp.exp(wt)[:, :, None, :] + sa[..., None] * bt[:, :, None, :] + vt[..., None] * kt[:, :, None, :]
        return s, jnp.einsum('bhij,bhj->bhi', s, rt)

    xs = tuple(jnp.moveaxis(t, 1, 0) for t in (r, logw, k, v, a, b))
    s, y = lax.scan(step, s0, xs)
    return jnp.moveaxis(y, 0, 1), s


def _layer(l, x, p_l, pos, v_first, st_hg, st_rw, st_shift, st_rn, lb, log_gamma, prm):
    bsz, t, _ = x.shape
    f32 = jnp.float32
    h = rmsnorm(x, prm['norm_pre'][l])
    z = h @ prm['w_in'][l]
    z_hg, z_rw, z_rn, z_gate, z_merge = _split(z, IN_SIZES)

    hq, hf, hi = _split(z_hg.astype(f32), (BR_W, BR_W, BR_W))
    lb_l = lb[l]
    sig_f = jax.nn.sigmoid(hf)
    f_gate = lb_l + (1.0 - lb_l) * sig_f
    log_f = jnp.log(jnp.maximum(f_gate, F_MIN))
    k_hg = (1.0 - lb_l) * (1.0 - sig_f)
    q_hg = jax.nn.silu(hq) * HG_DK ** -0.5
    o_hg, s_hg = gla_chunked(_heads(q_hg, HG_HEADS), _heads(k_hg, HG_HEADS), _heads(hi, HG_HEADS),
                             _heads(log_f, HG_HEADS), st_hg.astype(f32))
    y_hg = rmsnorm(o_hg, prm['hg_norm'][l]).reshape(bsz, t, BR_W)

    u, shift_last = token_shift(z_rw, st_shift.astype(z_rw.dtype), prm['rw_mu'][l])
    r, wlo, k, v, alo = _split(u.astype(f32), (BR_W, RW_DECAY_LORA, BR_W, BR_W, RW_AAA_LORA))
    w = -jax.nn.softplus(-(prm['rw_w0'][l] + jnp.tanh(wlo) @ prm['rw_w2'][l])) - 0.5
    a = jax.nn.sigmoid(prm['rw_a0'][l] + alo @ prm['rw_a2'][l])
    kk = _heads(k * prm['rw_k_k'][l], RW_HEADS)
    kk = kk / jnp.maximum(jnp.sqrt(jnp.sum(kk * kk, axis=-1, keepdims=True)), 1e-12)
    k = k * (1.0 + (a - 1.0) * prm['rw_k_a'][l])
    if l == 0:
        v_first = v
    else:
        v = v + (v_first - v) * jax.nn.sigmoid(prm['rw_v0'][l - 1] + (v @ prm['rw_v1'][l - 1]) @ prm['rw_v2'][l - 1])
    rh, kh, vh, ah = (_heads(r, RW_HEADS), _heads(k, RW_HEADS), _heads(v, RW_HEADS), _heads(a, RW_HEADS))
    y_rw, s_rw = rwkv7_scan(rh, -jnp.exp(_heads(w, RW_HEADS)), kh, vh, -kk, kk * ah, st_rw.astype(f32))
    mu = jnp.mean(y_rw, axis=-1, keepdims=True)
    var = jnp.mean(jnp.square(y_rw - mu), axis=-1, keepdims=True)
    y_rw = (y_rw - mu) * lax.rsqrt(var + RW_GN_EPS) * _heads(prm['rw_ln_w'][l], RW_HEADS) + _heads(prm['rw_ln_b'][l], RW_HEADS)
    y_rw = y_rw + jnp.sum(rh * kh * prm['rw_r_k'][l], axis=-1, keepdims=True) * vh
    y_rw = y_rw.reshape(bsz, t, BR_W)

    rq, rk, rv = _split(z_rn.astype(f32), (BR_W, BR_W, BR_W))
    q_rn = rotary(_heads(rq, RN_HEADS), pos)
    k_rn = rotary(_heads(rk, RN_HEADS), pos) * RN_DK ** -0.5
    o_rn, s_rn = retention_chunked(q_rn, k_rn, _heads(rv, RN_HEADS), log_gamma, st_rn.astype(f32))
    y_rn = rmsnorm(o_rn).reshape(bsz, t, BR_W)

    ys = jnp.stack([y_hg, y_rw, y_rn], axis=2).astype(x.dtype)
    ys = ys * jax.nn.silu(z_gate.reshape(bsz, t, N_BRANCH, BR_W))
    proj = jnp.einsum('btnc,ncd->btnd', ys, prm['w_branch'][l])
    m = jnp.einsum('btnd,btnd->btd', jax.nn.sigmoid(z_merge.reshape(bsz, t, N_BRANCH, D_MODEL)), proj)
    x = x + rmsnorm(m @ prm['w_out'][l], prm['norm_post'][l])

    e = rmsnorm(p_l @ prm['ple_up'][l], prm['ple_norm'][l])
    x = x + e * jax.nn.sigmoid(x @ prm['ple_gate'][l])
    return x, v_first, (s_hg, s_rw, shift_last, s_rn)


def _trunk(x, p, st_hg, st_rw, st_shift, st_rn, pos, prm):
    sm = jax.nn.softmax(prm['hg_lower_bounds'].astype(jnp.float32), axis=0)
    lb = jnp.cumsum(sm, axis=0) - sm[0]
    log_gamma = jnp.log(1.0 - 2.0 ** (-5.0 - jnp.arange(RN_HEADS, dtype=jnp.float32)))
    v_first = None
    outs = []
    for l in range(DEPTH):
        x, v_first, st = _layer(l, x, p[l], pos, v_first, st_hg[l], st_rw[l], st_shift[l], st_rn[l], lb, log_gamma, prm)
        outs.append(st)
    new = [jnp.stack([o[i] for o in outs], axis=0).astype(x.dtype) for i in range(4)]
    return x, new[0], new[1], new[2], new[3]


def setup_inputs(seed: int = 0) -> dict:
    key = jax.random.key(seed)
    ks = jax.random.split(key, 32)
    f32 = jnp.float32

    def nrm(i, shape, s):
        return jax.random.normal(ks[i], shape, f32) * s

    def uni(i, shape, lo, hi):
        return jax.random.uniform(ks[i], shape, f32, lo, hi)

    L = DEPTH
    return {
        'x_prompt': nrm(0, (BATCH, SEQ, D_MODEL), 1.0),
        'x_sample': nrm(1, (DEC_BATCH, DEC_SEQ, D_MODEL), 1.0),
        'state_hgrn': nrm(2, (L, DEC_BATCH, HG_HEADS, HG_DK, HG_DV), 0.5),
        'state_rwkv': nrm(3, (L, DEC_BATCH, RW_HEADS, RW_N, RW_N), 0.3),
        'state_shift': nrm(4, (L, DEC_BATCH, RW_SHIFT_W), 1.0),
        'state_ret': nrm(5, (L, DEC_BATCH, RN_HEADS, RN_DK, RN_DV), 1.0),
        'p_prompt': nrm(6, (L, BATCH, SEQ, PLE_DIM), 1.0),
        'p_sample': nrm(7, (L, DEC_BATCH, DEC_SEQ, PLE_DIM), 1.0),
        'norm_pre': 1.0 + nrm(8, (L, D_MODEL), 0.02),
        'w_in': nrm(9, (L, D_MODEL, D_IN), D_MODEL ** -0.5),
        'hg_lower_bounds': nrm(10, (L, BR_W), 0.1),
        'hg_norm': 1.0 + nrm(11, (L, HG_DV), 0.02),
        'rw_mu': uni(12, (L, RW_SHIFT_W), 0.0, 1.0),
        'rw_w0': uni(13, (L, BR_W), -6.0, -1.0),
        'rw_w2': nrm(14, (L, RW_DECAY_LORA, BR_W), 0.5 * RW_DECAY_LORA ** -0.5),
        'rw_a0': nrm(15, (L, BR_W), 0.1),
        'rw_a2': nrm(16, (L, RW_AAA_LORA, BR_W), 0.5 * RW_AAA_LORA ** -0.5),
        'rw_k_k': 0.85 + nrm(17, (L, BR_W), 0.05),
        'rw_k_a': 1.0 + nrm(18, (L, BR_W), 0.05),
        'rw_v0': 1.0 + nrm(19, (L - 1, BR_W), 0.1),
        'rw_v1': nrm(20, (L - 1, BR_W, RW_MV_LORA), BR_W ** -0.5),
        'rw_v2': nrm(21, (L - 1, RW_MV_LORA, BR_W), 0.5 * RW_MV_LORA ** -0.5),
        'rw_r_k': nrm(22, (L, RW_HEADS, RW_N), 0.1),
        'rw_ln_w': 1.0 + nrm(23, (L, BR_W), 0.02),
        'rw_ln_b': nrm(24, (L, BR_W), 0.02),
        'w_branch': nrm(25, (L, N_BRANCH, BR_W, D_MODEL), BR_W ** -0.5),
        'w_out': nrm(26, (L, D_MODEL, D_MODEL), D_MODEL ** -0.5),
        'norm_post': 1.0 + nrm(27, (L, D_MODEL), 0.02),
        'ple_up': nrm(28, (L, PLE_DIM, D_MODEL), PLE_DIM ** -0.5),
        'ple_norm': 1.0 + nrm(29, (L, D_MODEL), 0.02),
        'ple_gate': nrm(30, (L, D_MODEL, D_MODEL), D_MODEL ** -0.5),
    }


def reference(x_prompt, x_sample, state_hgrn, state_rwkv, state_shift, state_ret, p_prompt, p_sample,
              norm_pre, w_in, hg_lower_bounds, hg_norm, rw_mu, rw_w0, rw_w2, rw_a0, rw_a2, rw_k_k, rw_k_a,
              rw_v0, rw_v1, rw_v2, rw_r_k, rw_ln_w, rw_ln_b, w_branch, w_out, norm_post, ple_up, ple_norm, ple_gate):
    prm = dict(norm_pre=norm_pre, w_in=w_in, hg_lower_bounds=hg_lower_bounds, hg_norm=hg_norm, rw_mu=rw_mu,
               rw_w0=rw_w0, rw_w2=rw_w2, rw_a0=rw_a0, rw_a2=rw_a2, rw_k_k=rw_k_k, rw_k_a=rw_k_a, rw_v0=rw_v0,
               rw_v1=rw_v1, rw_v2=rw_v2, rw_r_k=rw_r_k, rw_ln_w=rw_ln_w, rw_ln_b=rw_ln_b, w_branch=w_branch,
               w_out=w_out, norm_post=norm_post, ple_up=ple_up, ple_norm=ple_norm, ple_gate=ple_gate)
    f32 = jnp.float32
    bp, tp = x_prompt.shape[0], x_prompt.shape[1]
    ts = x_sample.shape[1]
    z_hg = jnp.zeros((DEPTH, bp, HG_HEADS, HG_DK, HG_DV), f32)
    z_rw = jnp.zeros((DEPTH, bp, RW_HEADS, RW_N, RW_N), f32)
    z_sh = jnp.zeros((DEPTH, bp, RW_SHIFT_W), x_prompt.dtype)
    z_rn = jnp.zeros((DEPTH, bp, RN_HEADS, RN_DK, RN_DV), f32)
    y_prompt, hg_p, rw_p, sh_p, rn_p = _trunk(x_prompt, p_prompt, z_hg, z_rw, z_sh, z_rn,
                                              jnp.arange(tp), prm)
    y_sample, hg_s, rw_s, sh_s, rn_s = _trunk(x_sample, p_sample, state_hgrn, state_rwkv, state_shift, state_ret,
                                              PAST_LEN + jnp.arange(ts), prm)
    return (y_prompt, y_sample, hg_p, hg_s, rw_p, rw_s, sh_p, sh_s, rn_p, rn_s)
```

```python
import functools

import numpy as np
import jax
import jax.numpy as jnp
from jax import lax
from jax.experimental import pallas as pl
from jax.experimental.pallas import tpu as pltpu

F32 = jnp.float32
BF16 = jnp.bfloat16
HIGHEST = lax.Precision.HIGHEST

D_MODEL = 2048
BATCH, SEQ = 4, 2048
DEC_BATCH, DEC_SEQ = 128, 8
DEPTH = 2
PAST_LEN = 16384
N_BRANCH = 3
BR_W = 1024
HG_HEADS, HG_DK, HG_DV = 8, 128, 128
F_MIN = 1e-30
RW_HEADS, RW_N = 16, 64
RW_LORA = 96
RW_MV_LORA = 64
RW_GN_EPS = 64e-5
RN_HEADS, RN_DK, RN_DV = 4, 256, 256
ROPE_BASE = 10000.0
CHUNK = 64
PLE_DIM = 256
EPS = 1e-6

LANES = 128
SUBLANES = 8
MXU_DIM = 256

LORA_PAD = LANES
Z_HG = 0
Z_RKV = Z_HG + 3 * BR_W
Z_RN = Z_RKV + 3 * BR_W
Z_GATE = Z_RN + 3 * BR_W
Z_MERGE = Z_GATE + N_BRANCH * BR_W
Z_LORA = Z_MERGE + N_BRANCH * D_MODEL
Z_USED = Z_LORA + 2 * LORA_PAD
INPROJ_TN = 512
Z_W = -(-Z_USED // INPROJ_TN) * INPROJ_TN
INPROJ_TM = 1024

N_PROMPT = BATCH * SEQ
N_SAMPLE = DEC_BATCH * DEC_SEQ
N_ROWS = N_PROMPT + N_SAMPLE


def _bf(x):
    return x.astype(BF16)


def _dot(a, b):
    return jnp.dot(_bf(a), _bf(b), preferred_element_type=F32)


def _dot_nt(a, b):
    return lax.dot_general(_bf(a), _bf(b), (((1,), (1,)), ((), ())), preferred_element_type=F32)


def _dot_tn(a, b):
    return lax.dot_general(_bf(a), _bf(b), (((0,), (0,)), ((), ())), preferred_element_type=F32)


def _split_dot(x, g):
    hi = _bf(x)
    lo = _bf(x - hi.astype(F32))
    return (jnp.dot(hi, g, preferred_element_type=F32) + jnp.dot(lo, g, preferred_element_type=F32))


def _head_ones(width, head):
    r = lax.broadcasted_iota(jnp.int32, (width, width), 0) // head
    c = lax.broadcasted_iota(jnp.int32, (width, width), 1) // head
    return (r == c).astype(BF16)


def _rms(x):
    return x * lax.rsqrt(jnp.mean(x * x, axis=-1, keepdims=True) + EPS)


class _Group:
    def __init__(self, row0, batch, seq, pos0, has_state):
        self.row0, self.batch, self.seq, self.pos0, self.has_state = row0, batch, seq, pos0, has_state
        self.rows = batch * seq


PROMPT = _Group(0, BATCH, SEQ, 0, False)
SAMPLE = _Group(N_PROMPT, DEC_BATCH, DEC_SEQ, PAST_LEN, True)


def _inproj_kernel(x_ref, g_ref, w_ref, o_ref, h_ref):
    @pl.when(pl.program_id(1) == 0)
    def _():
        h_ref[...] = _bf(_rms(x_ref[...]) * g_ref[...])

    o_ref[...] = jnp.dot(h_ref[...], w_ref[...], preferred_element_type=F32)


def _inproj(x, g, w):
    n = x.shape[0]
    return pl.pallas_call(
        _inproj_kernel,
        grid=(n // INPROJ_TM, Z_W // INPROJ_TN),
        in_specs=[
            pl.BlockSpec((INPROJ_TM, D_MODEL), lambda i, j: (i, 0)),
            pl.BlockSpec((1, D_MODEL), lambda i, j: (0, 0)),
            pl.BlockSpec((D_MODEL, INPROJ_TN), lambda i, j: (0, j)),
        ],
        out_specs=pl.BlockSpec((INPROJ_TM, INPROJ_TN), lambda i, j: (i, j)),
        out_shape=jax.ShapeDtypeStruct((n, Z_W), F32),
        scratch_shapes=[pltpu.VMEM((INPROJ_TM, D_MODEL), BF16)],
        compiler_params=pltpu.CompilerParams(
            dimension_semantics=("parallel", "arbitrary"), vmem_limit_bytes=48 << 20),
        name="inproj",
    )(x, g, w)


def _pack_w_in(w):
    o_rw = 3 * BR_W
    r = w[:, o_rw:o_rw + BR_W]
    wlo = w[:, o_rw + BR_W:o_rw + BR_W + RW_LORA]
    k = w[:, o_rw + BR_W + RW_LORA:o_rw + 2 * BR_W + RW_LORA]
    v = w[:, o_rw + 2 * BR_W + RW_LORA:o_rw + 3 * BR_W + RW_LORA]
    alo = w[:, o_rw + 3 * BR_W + RW_LORA:o_rw + 3 * BR_W + 2 * RW_LORA]
    rest = w[:, o_rw + 3 * BR_W + 2 * RW_LORA:]
    zl = jnp.zeros((w.shape[0], LORA_PAD - RW_LORA), w.dtype)
    zt = jnp.zeros((w.shape[0], Z_W - Z_USED), w.dtype)
    return _bf(jnp.concatenate([w[:, :o_rw], r, k, v, rest, wlo, zl, alo, zl, zt], axis=1))


def _pack_shift_cols(a):
    r = a[..., :BR_W]
    wlo = a[..., BR_W:BR_W + RW_LORA]
    k = a[..., BR_W + RW_LORA:2 * BR_W + RW_LORA]
    v = a[..., 2 * BR_W + RW_LORA:3 * BR_W + RW_LORA]
    alo = a[..., 3 * BR_W + RW_LORA:]
    zl = jnp.zeros(a.shape[:-1] + (LORA_PAD - RW_LORA,), a.dtype)
    return jnp.concatenate([r, k, v], axis=-1), jnp.concatenate([wlo, zl, alo, zl], axis=-1)


def _unpack_shift_rows(zr):
    return jnp.concatenate([
        zr[:, Z_RKV:Z_RKV + BR_W],
        zr[:, Z_LORA:Z_LORA + RW_LORA],
        zr[:, Z_RKV + BR_W:Z_RKV + 3 * BR_W],
        zr[:, Z_LORA + LORA_PAD:Z_LORA + LORA_PAD + RW_LORA],
    ], axis=1)


def _gla_level_matrix(c):
    nlv = c.bit_length() - 1
    m = np.zeros(((nlv + 1) * c, c), np.float32)
    for r in range(c):
        m[r, :r + 1] = 1.0
        for p in range(nlv):
            bd = ((r >> (p + 1)) << (p + 1)) + (1 << p) - 1
            if (r >> p) & 1:
                m[(p + 1) * c + r, bd + 1:r + 1] = 1.0
            else:
                m[(p + 1) * c + r, r + 1:bd + 1] = 1.0
    return m


def _hgrn_kernel(*refs, c, nb, layer, has_state):
    if has_state:
        hq_ref, hf_ref, hi_ref, lbp_ref, gn_ref, ms_ref, s0_ref, y_ref, so_ref, s_scr = refs
    else:
        hq_ref, hf_ref, hi_ref, lbp_ref, gn_ref, ms_ref, y_ref, so_ref, s_scr = refs
    nlv = c.bit_length() - 1
    ci = pl.program_id(2)

    @pl.when(ci == 0)
    def _():
        if has_state:
            s_scr[...] = s0_ref[:, 0]
        else:
            s_scr[...] = jnp.zeros_like(s_scr)

    lbp = lbp_ref[...]
    e = jnp.exp(lbp - jnp.max(lbp, axis=0, keepdims=True))
    sm = e / jnp.sum(e, axis=0, keepdims=True)
    lb = jnp.sum(sm[0:layer + 1], axis=0, keepdims=True) - sm[0:1]

    ti = lax.broadcasted_iota(jnp.int32, (c, c), 0)
    si = lax.broadcasted_iota(jnp.int32, (c, c), 1)
    txs = ti ^ si
    lower = ti > si
    ones_c = jnp.ones((c, HG_DV), F32)

    for ib in range(nb):
        rows = pl.ds(ib * c, c)
        hq, hf, v = hq_ref[rows, :], hf_ref[rows, :], hi_ref[rows, :]
        sig = jax.nn.sigmoid(hf)
        f_gate = lb + (1.0 - lb) * sig
        logf = jnp.log(jnp.maximum(f_gate, F_MIN))
        k = (1.0 - lb) * (1.0 - sig)
        q = jax.nn.silu(hq) * HG_DK ** -0.5
        lv = jnp.dot(ms_ref[...], logf, precision=HIGHEST, preferred_element_type=F32)
        b = lv[0:c]
        att = jnp.where(ti == si, jnp.sum(q * k, axis=-1, keepdims=True), 0.0)
        for p in range(nlv):
            ex = jnp.exp(lv[(p + 1) * c:(p + 2) * c])
            att = att + jnp.where(((txs >> p) == 1) & lower, _dot_nt(q * ex, k * ex), 0.0)
        s = s_scr[ib]
        o = _dot(att, v) + _dot(q * jnp.exp(b), s)
        bl = b[c - 1:c]
        bl_col = lax.dot_general(logf, ones_c, (((0,), (0,)), ((), ())), precision=HIGHEST,
                                 preferred_element_type=F32)
        s_scr[ib] = jnp.exp(bl_col) * s + _dot_tn(k * jnp.exp(bl - b), v)
        y_ref[rows, :] = _rms(o) * gn_ref[...]

    @pl.when(ci == pl.num_programs(2) - 1)
    def _():
        so_ref[:, 0] = s_scr[...]


def _hgrn(z, grp, layer, lower_bounds, gnorm, state):
    c = min(CHUNK, grp.seq)
    nb = 1 if grp.seq > c else 8
    nchunk = grp.seq // c
    blk = nb * c
    base = grp.row0 // blk
    hb = HG_DK // LANES

    def zspec(col0):
        return pl.BlockSpec((blk, HG_DK), lambda i, h, ci: (base + i * nchunk + ci, col0 // HG_DK + h * hb))

    in_specs = [zspec(Z_HG), zspec(Z_HG + BR_W), zspec(Z_HG + 2 * BR_W),
                pl.BlockSpec((DEPTH, HG_DK), lambda i, h, ci: (0, h)),
                pl.BlockSpec((1, HG_DV), lambda i, h, ci: (0, 0)),
                pl.BlockSpec(((c.bit_length()) * c, c), lambda i, h, ci: (0, 0))]
    args = [z, z, z, lower_bounds, gnorm, jnp.asarray(_gla_level_matrix(c))]
    st_spec = pl.BlockSpec((nb, 1, HG_DK, HG_DV), lambda i, h, ci: (i, h, 0, 0))
    if grp.has_state:
        in_specs.append(st_spec)
        args.append(state)
    return pl.pallas_call(
        functools.partial(_hgrn_kernel, c=c, nb=nb, layer=layer, has_state=grp.has_state),
        grid=(grp.batch // nb, HG_HEADS, nchunk),
        in_specs=in_specs,
        out_specs=[pl.BlockSpec((blk, HG_DV), lambda i, h, ci: (i * nchunk + ci, h)), st_spec],
        out_shape=[jax.ShapeDtypeStruct((grp.rows, BR_W), F32),
                   jax.ShapeDtypeStruct((grp.batch, HG_HEADS, HG_DK, HG_DV), F32)],
        scratch_shapes=[pltpu.VMEM((nb, HG_DK, HG_DV), F32)],
        compiler_params=pltpu.CompilerParams(
            dimension_semantics=("parallel", "parallel", "arbitrary")),
        name="hgrn_" + ("sample" if grp.has_state else "prompt"),
    )(*args)


def _rope_kernel(inv_ref, cos_ref, sin_ref, *, pos0):
    t = cos_ref.shape[0]
    pos = (pos0 + lax.broadcasted_iota(jnp.int32, (t, RN_DK // 2), 0)).astype(F32)
    ang = pos * inv_ref[...]
    cos_ref[...] = jnp.cos(ang)
    sin_ref[...] = jnp.sin(ang)


def _rope_tables(grp):
    inv = ROPE_BASE ** (-jnp.arange(0, RN_DK, 2, dtype=F32) / RN_DK)
    shp = jax.ShapeDtypeStruct((grp.seq, RN_DK // 2), F32)
    return pl.pallas_call(functools.partial(_rope_kernel, pos0=grp.pos0), out_shape=[shp, shp],
                          name="rope")(inv[None, :])


def _ret_kernel(*refs, c, nb, has_state):
    if has_state:
        (q_ref, k_ref, v_ref, cos_ref, sin_ref, dm_ref, qd_ref, kd_ref, cd_ref, s0_ref,
         y_ref, so_ref, s_scr) = refs
    else:
        (q_ref, k_ref, v_ref, cos_ref, sin_ref, dm_ref, qd_ref, kd_ref, cd_ref,
         y_ref, so_ref, s_scr) = refs
    ci = pl.program_id(2)

    @pl.when(ci == 0)
    def _():
        if has_state:
            s_scr[...] = s0_ref[:, 0]
        else:
            s_scr[...] = jnp.zeros_like(s_scr)

    cos, sin = cos_ref[...], sin_ref[...]
    half = RN_DK // 2

    def rot(x):
        x1, x2 = x[:, :half], x[:, half:]
        return jnp.concatenate([x1 * cos - x2 * sin, x1 * sin + x2 * cos], axis=1)

    for ib in range(nb):
        rows = pl.ds(ib * c, c)
        q = rot(q_ref[rows, :])
        k = rot(k_ref[rows, :]) * RN_DK ** -0.5
        v = v_ref[rows, :]
        s = s_scr[ib]
        att = _dot_nt(q, k) * dm_ref[0]
        o = _dot(att, v) + _dot(q, s) * qd_ref[0]
        s_scr[ib] = cd_ref[0] * s + _dot_tn(k * kd_ref[0], v)
        y_ref[rows, :] = _rms(o)

    @pl.when(ci == pl.num_programs(2) - 1)
    def _():
        so_ref[:, 0] = s_scr[...]


def _retention(z, grp, tables, state):
    c = min(CHUNK, grp.seq)
    nb = 1 if grp.seq > c else 4
    nchunk = grp.seq // c
    blk = nb * c
    base = grp.row0 // blk
    log_gamma = jnp.log(1.0 - 2.0 ** (-5.0 - jnp.arange(RN_HEADS, dtype=F32)))
    j = jnp.arange(c, dtype=F32)
    rel = j[:, None] - j[None, :]
    dmat = jnp.where(rel >= 0, jnp.exp(log_gamma[:, None, None] * jnp.maximum(rel, 0.0)), 0.0)
    q_dec = jnp.exp(log_gamma[:, None] * (j + 1.0))[..., None]
    k_dec = jnp.exp(log_gamma[:, None] * (c - 1.0 - j))[..., None]
    c_dec = jnp.exp(log_gamma * c)[:, None, None]

    def zspec(col0):
        return pl.BlockSpec((blk, RN_DK), lambda i, h, ci: (base + i * nchunk + ci, col0 // RN_DK + h))

    tspec = pl.BlockSpec((c, RN_DK // 2), lambda i, h, ci: (ci, 0))
    in_specs = [zspec(Z_RN), zspec(Z_RN + BR_W), zspec(Z_RN + 2 * BR_W), tspec, tspec,
                pl.BlockSpec((1, c, c), lambda i, h, ci: (h, 0, 0)),
                pl.BlockSpec((1, c, 1), lambda i, h, ci: (h, 0, 0)),
                pl.BlockSpec((1, c, 1), lambda i, h, ci: (h, 0, 0)),
                pl.BlockSpec((1, 1, 1), lambda i, h, ci: (h, 0, 0))]
    args = [z, z, z, tables[0], tables[1], dmat, q_dec, k_dec, c_dec]
    st_spec = pl.BlockSpec((nb, 1, RN_DK, RN_DV), lambda i, h, ci: (i, h, 0, 0))
    if grp.has_state:
        in_specs.append(st_spec)
        args.append(state)
    return pl.pallas_call(
        functools.partial(_ret_kernel, c=c, nb=nb, has_state=grp.has_state),
        grid=(grp.batch // nb, RN_HEADS, nchunk),
        in_specs=in_specs,
        out_specs=[pl.BlockSpec((blk, RN_DV), lambda i, h, ci: (i * nchunk + ci, h)), st_spec],
        out_shape=[jax.ShapeDtypeStruct((grp.rows, BR_W), F32),
                   jax.ShapeDtypeStruct((grp.batch, RN_HEADS, RN_DK, RN_DV), F32)],
        scratch_shapes=[pltpu.VMEM((nb, RN_DK, RN_DV), F32)],
        compiler_params=pltpu.CompilerParams(
            dimension_semantics=("parallel", "parallel", "arbitrary")),
        name="ret_" + ("sample" if grp.has_state else "prompt"),
    )(*args)


RW_PREP_ROWS = 256


def _rw_prep_kernel(*refs, tt, nb, has_state, with_vmix):
    refs = list(refs)
    zr_ref, zl_ref = refs[:2]
    pos = 2
    if has_state:
        sr_ref, sl_ref = refs[pos:pos + 2]
        pos += 2
    (mur_ref, mul_ref, w0_ref, w2_ref, a0_ref, a2_ref, kk_ref, ka_ref) = refs[pos:pos + 8]
    pos += 8
    if with_vmix:
        v0_ref, v1_ref, v2_ref, vf_ref = refs[pos:pos + 4]
        pos += 4
    r_ref, d_ref, k_ref, v_ref, a_ref, b_ref = refs[pos:pos + 6]
    pr_scr, pl_scr = refs[pos + 6:pos + 8]
    ti = pl.program_id(1)
    rows = nb * tt

    zr, zl = zr_ref[...], zl_ref[...]
    first = (lax.broadcasted_iota(jnp.int32, (rows, 1), 0) % tt) == 0

    def prev_rows(z, st_ref, scr):
        w = z.shape[1]
        if nb > 1:
            st = st_ref[...]
            return jnp.broadcast_to(st[:, None, :], (nb, tt, w)).reshape(rows, w)

        @pl.when(ti == 0)
        def _():
            scr[0:1, :] = st_ref[...] if has_state else jnp.zeros((1, w), F32)

        return scr[0:1, :]

    def shifted(z, st_ref, scr, mu):
        prev = jnp.where(first, prev_rows(z, st_ref, scr), pltpu.roll(z, 1, 0))
        scr[0:1, :] = z[rows - 1:rows, :]
        return z + (prev - z) * mu

    ur = shifted(zr, sr_ref if has_state else None, pr_scr, mur_ref[...])
    ul = shifted(zl, sl_ref if has_state else None, pl_scr, mul_ref[...])

    r, k, v = ur[:, :BR_W], ur[:, BR_W:2 * BR_W], ur[:, 2 * BR_W:]
    wlo, alo = ul[:, :LORA_PAD], ul[:, LORA_PAD:]
    w = -jax.nn.softplus(-(w0_ref[...] + _dot(jnp.tanh(wlo), w2_ref[...]))) - 0.5
    a = jax.nn.sigmoid(a0_ref[...] + _dot(alo, a2_ref[...]))
    kk = k * kk_ref[...]
    ones = _head_ones(MXU_DIM, RW_N)
    kk2 = kk * kk
    nrm2 = jnp.concatenate(
        [_split_dot(kk2[:, g * MXU_DIM:(g + 1) * MXU_DIM], ones) for g in range(BR_W // MXU_DIM)], axis=1)
    kk = kk / jnp.maximum(jnp.sqrt(nrm2), 1e-12)
    k = k * (1.0 + (a - 1.0) * ka_ref[...])
    if with_vmix:
        mix = jax.nn.sigmoid(v0_ref[...] + _dot(_dot(v, v1_ref[...]), v2_ref[...]))
        v = v + (vf_ref[...] - v) * mix
    r_ref[...] = r
    d_ref[...] = jnp.exp(-jnp.exp(w))
    k_ref[...] = k
    v_ref[...] = v
    a_ref[...] = -kk
    b_ref[...] = kk * a


def _rw_prep(z, grp, prm, state_rkv, state_lora, v_first):
    tt = min(RW_PREP_ROWS, grp.seq)
    nb = RW_PREP_ROWS // tt
    ntile = grp.seq // tt
    rows = nb * tt
    base = grp.row0 // rows
    with_vmix = v_first is not None
    w_rkv, w_lora = 3 * BR_W, 2 * LORA_PAD

    def rowmap(i, t):
        return base + i * ntile + t

    def const(shape):
        return pl.BlockSpec(shape, lambda i, t: (0,) * len(shape))

    in_specs = [pl.BlockSpec((rows, w_rkv), lambda i, t: (rowmap(i, t), Z_RKV // w_rkv)),
                pl.BlockSpec((rows, w_lora), lambda i, t: (rowmap(i, t), Z_LORA // w_lora))]
    args = [z, z]
    if grp.has_state:
        in_specs += [pl.BlockSpec((nb, w_rkv), lambda i, t: (i, 0)),
                     pl.BlockSpec((nb, w_lora), lambda i, t: (i, 0))]
        args += [state_rkv, state_lora]
    in_specs += [const((1, w_rkv)), const((1, w_lora)), const((1, BR_W)), const((LORA_PAD, BR_W)),
                 const((1, BR_W)), const((LORA_PAD, BR_W)), const((1, BR_W)), const((1, BR_W))]
    args += [prm['mu_rkv'], prm['mu_lora'], prm['w0'], prm['w2'], prm['a0'], prm['a2'], prm['k_k'], prm['k_a']]
    out_spec = pl.BlockSpec((rows, BR_W), lambda i, t: (i * ntile + t, 0))
    if with_vmix:
        in_specs += [const((1, BR_W)), const((BR_W, LANES)), const((LANES, BR_W)), out_spec]
        args += [prm['v0'], prm['v1'], prm['v2'], v_first]
    out = jax.ShapeDtypeStruct((grp.rows, BR_W), F32)
    return pl.pallas_call(
        functools.partial(_rw_prep_kernel, tt=tt, nb=nb, has_state=grp.has_state, with_vmix=with_vmix),
        grid=(grp.batch // nb, ntile),
        in_specs=in_specs,
        out_specs=[out_spec] * 6,
        out_shape=[out] * 6,
        scratch_shapes=[pltpu.VMEM((SUBLANES, w_rkv), F32), pltpu.VMEM((SUBLANES, w_lora), F32)],
        compiler_params=pltpu.CompilerParams(
            dimension_semantics=("parallel", "arbitrary"), vmem_limit_bytes=48 << 20),
        name="rw_prep_" + ("sample" if grp.has_state else "prompt"),
    )(*args)


RW_TILE = MXU_DIM
RW_TILE_HEADS = RW_TILE // RW_N
RW_SCAN_NB = 4
RW_SCAN_NG = 2
RW_SCAN_TT = 128


def _rw_scan_kernel(*refs, tt, nb, ng, has_state):
    if has_state:
        (r_ref, d_ref, k_ref, v_ref, a_ref, b_ref, lw_ref, lb_ref, rk_ref, s0_ref,
         y_ref, so_ref, s_scr, y_scr) = refs
    else:
        (r_ref, d_ref, k_ref, v_ref, a_ref, b_ref, lw_ref, lb_ref, rk_ref,
         y_ref, so_ref, s_scr, y_scr) = refs
    ti = pl.program_id(2)
    chains = [(ib, g) for ib in range(nb) for g in range(ng)]

    @pl.when(ti == 0)
    def _():
        for ib, g in chains:
            if has_state:
                s_scr[ib, g] = jnp.concatenate(
                    [s0_ref[ib, g * RW_TILE_HEADS + h] for h in range(RW_TILE_HEADS)], axis=1)
            else:
                s_scr[ib, g] = jnp.zeros((RW_N, RW_TILE), F32)

    ones = _head_ones(RW_TILE, RW_N)
    eye = (lax.broadcasted_iota(jnp.int32, (RW_N, RW_TILE), 0)
           == lax.broadcasted_iota(jnp.int32, (RW_N, RW_TILE), 1) % RW_N).astype(F32)

    def row(ref, ib, g, t):
        return ref[ib, pl.ds(t, 1), g * RW_TILE:(g + 1) * RW_TILE]

    def step(t, carry):
        lhs = []
        for ib, g in chains:
            p = s_scr[ib, g] * row(a_ref, ib, g, t)
            hi = _bf(p)
            lhs += [hi, _bf(p - hi.astype(F32)), _bf(eye * row(v_ref, ib, g, t))]
        res = jnp.dot(jnp.concatenate(lhs, axis=0), ones, preferred_element_type=F32)
        lhs = []
        for n, (ib, g) in enumerate(chains):
            o = 3 * n * RW_N
            sa = res[o:o + RW_N] + res[o + RW_N:o + 2 * RW_N]
            vcol = res[o + 2 * RW_N:o + 3 * RW_N]
            s = (s_scr[ib, g] * row(d_ref, ib, g, t) + sa * row(b_ref, ib, g, t)
                 + vcol * row(k_ref, ib, g, t))
            s_scr[ib, g] = s
            lhs.append(_bf(s * row(r_ref, ib, g, t)))
        res = jnp.dot(jnp.concatenate(lhs, axis=0), ones, preferred_element_type=F32)
        for n, (ib, g) in enumerate(chains):
            yb = res[n * RW_N:(n + 1) * RW_N]
            y_scr[ib, pl.ds(t, 1), g * RW_TILE:(g + 1) * RW_TILE] = jnp.sum(yb * eye, axis=0, keepdims=True)
        return carry

    lax.fori_loop(0, tt, step, 0)

    for ib, g in chains:
        sl = slice(g * RW_TILE, (g + 1) * RW_TILE)
        y = y_scr[ib, :, sl]
        mu = _split_dot(y, ones) * (1.0 / RW_N)
        dlt = y - mu
        var = _split_dot(dlt * dlt, ones) * (1.0 / RW_N)
        yn = dlt * lax.rsqrt(var + RW_GN_EPS) * lw_ref[:, sl] + lb_ref[:, sl]
        bonus = _split_dot(r_ref[ib, :, sl] * k_ref[ib, :, sl] * rk_ref[:, sl], ones)
        y_ref[ib, :, sl] = yn + bonus * v_ref[ib, :, sl]

    @pl.when(ti == pl.num_programs(2) - 1)
    def _():
        for ib, g in chains:
            s = s_scr[ib, g]
            for h in range(RW_TILE_HEADS):
                so_ref[ib, g * RW_TILE_HEADS + h] = s[:, h * RW_N:(h + 1) * RW_N]


def _rw_scan(seqs, grp, ln_w, ln_b, r_k, state):
    tt = min(RW_SCAN_TT, grp.seq)
    nb, ng = RW_SCAN_NB, RW_SCAN_NG
    width = ng * RW_TILE
    nh = ng * RW_TILE_HEADS
    seq_spec = pl.BlockSpec((nb, tt, width), lambda i, j, t: (i, t, j))
    par_spec = pl.BlockSpec((1, width), lambda i, j, t: (0, j))
    st_spec = pl.BlockSpec((nb, nh, RW_N, RW_N), lambda i, j, t: (i, j, 0, 0))
    in_specs = [seq_spec] * 6 + [par_spec] * 3
    args = [s.reshape(grp.batch, grp.seq, BR_W) for s in seqs] + [ln_w, ln_b, r_k]
    if grp.has_state:
        in_specs.append(st_spec)
        args.append(state)
    y, s_out = pl.pallas_call(
        functools.partial(_rw_scan_kernel, tt=tt, nb=nb, ng=ng, has_state=grp.has_state),
        grid=(grp.batch // nb, BR_W // width, grp.seq // tt),
        in_specs=in_specs,
        out_specs=[seq_spec, st_spec],
        out_shape=[jax.ShapeDtypeStruct((grp.batch, grp.seq, BR_W), F32),
                   jax.ShapeDtypeStruct((grp.batch, RW_HEADS, RW_N, RW_N), F32)],
        scratch_shapes=[pltpu.VMEM((nb, ng, RW_N, RW_TILE), F32), pltpu.VMEM((nb, tt, width), F32)],
        compiler_params=pltpu.CompilerParams(
            dimension_semantics=("parallel", "parallel", "arbitrary"), vmem_limit_bytes=48 << 20),
        name="rw_scan_" + ("sample" if grp.has_state else "prompt"),
    )(*args)
    return y.reshape(grp.rows, BR_W), s_out


MERGE_TM = 256


def _merge_kernel(y0_ref, y1_ref, y2_ref, g_ref, zm_ref, wb_ref, m_ref):
    acc = None
    for n, y_ref in enumerate((y0_ref, y1_ref, y2_ref)):
        ys = y_ref[...] * jax.nn.silu(g_ref[:, n * BR_W:(n + 1) * BR_W])
        proj = jnp.dot(_bf(ys), wb_ref[n], preferred_element_type=F32)
        term = jax.nn.sigmoid(zm_ref[:, n * D_MODEL:(n + 1) * D_MODEL]) * proj
        acc = term if acc is None else acc + term
    m_ref[...] = _bf(acc)


def _merge(ys, z, wb):
    n = z.shape[0]
    yspec = pl.BlockSpec((MERGE_TM, BR_W), lambda i: (i, 0))
    gw, mw = N_BRANCH * BR_W, N_BRANCH * D_MODEL
    return pl.pallas_call(
        _merge_kernel,
        grid=(n // MERGE_TM,),
        in_specs=[yspec, yspec, yspec,
                  pl.BlockSpec((MERGE_TM, gw), lambda i: (i, Z_GATE // gw)),
                  pl.BlockSpec((MERGE_TM, mw), lambda i: (i, Z_MERGE // mw)),
                  pl.BlockSpec((N_BRANCH, BR_W, D_MODEL), lambda i: (0, 0, 0), pipeline_mode=pl.Buffered(1))],
        out_specs=pl.BlockSpec((MERGE_TM, D_MODEL), lambda i: (i, 0)),
        out_shape=jax.ShapeDtypeStruct((n, D_MODEL), BF16),
        compiler_params=pltpu.CompilerParams(
            dimension_semantics=("parallel",), vmem_limit_bytes=48 << 20),
        name="merge",
    )(*ys, z, z, wb)


POST_TM = 256


def _post_kernel(x_ref, m_ref, p_ref, wo_ref, np_ref, pu_ref, pn_ref, pg_ref, o_ref):
    x = x_ref[...] + _rms(jnp.dot(m_ref[...], wo_ref[...], preferred_element_type=F32)) * np_ref[...]
    e = _rms(jnp.dot(_bf(p_ref[...]), pu_ref[...], preferred_element_type=F32)) * pn_ref[...]
    o_ref[...] = x + e * jax.nn.sigmoid(jnp.dot(_bf(x), pg_ref[...], preferred_element_type=F32))


def _post(x, m, p, wo, norm_post, ple_up, ple_norm, ple_gate):
    n = x.shape[0]

    def rows(w):
        return pl.BlockSpec((POST_TM, w), lambda i: (i, 0))

    def const(shape):
        return pl.BlockSpec(shape, lambda i: (0, 0), pipeline_mode=pl.Buffered(1))

    return pl.pallas_call(
        _post_kernel,
        grid=(n // POST_TM,),
        in_specs=[rows(D_MODEL), rows(D_MODEL), rows(PLE_DIM), const((D_MODEL, D_MODEL)), const((1, D_MODEL)),
                  const((PLE_DIM, D_MODEL)), const((1, D_MODEL)), const((D_MODEL, D_MODEL))],
        out_specs=rows(D_MODEL),
        out_shape=jax.ShapeDtypeStruct((n, D_MODEL), F32),
        compiler_params=pltpu.CompilerParams(
            dimension_semantics=("parallel",), vmem_limit_bytes=48 << 20),
        name="post",
    )(x, m, p, wo, norm_post, ple_up, ple_norm, ple_gate)


def _pad_rows(a, rows):
    return jnp.concatenate([a, jnp.zeros((rows - a.shape[0],) + a.shape[1:], a.dtype)], axis=0)


def _pad_cols(a, cols):
    return jnp.concatenate([a, jnp.zeros(a.shape[:-1] + (cols - a.shape[-1],), a.dtype)], axis=-1)


def kernel(x_prompt, x_sample, state_hgrn, state_rwkv, state_shift, state_ret, p_prompt, p_sample,
           norm_pre, w_in, hg_lower_bounds, hg_norm, rw_mu, rw_w0, rw_w2, rw_a0, rw_a2, rw_k_k, rw_k_a,
           rw_v0, rw_v1, rw_v2, rw_r_k, rw_ln_w, rw_ln_b, w_branch, w_out, norm_post, ple_up, ple_norm, ple_gate):
    groups = (PROMPT, SAMPLE)
    x = jnp.concatenate([x_prompt.reshape(N_PROMPT, D_MODEL), x_sample.reshape(N_SAMPLE, D_MODEL)], axis=0)
    tables = [_rope_tables(g) for g in groups]
    v_first = [None, None]
    new_states = [[[], [], [], []] for _ in groups]

    for l in range(DEPTH):
        z = _inproj(x, norm_pre[l][None], _pack_w_in(w_in[l]))
        mu_rkv, mu_lora = _pack_shift_cols(rw_mu[l][None])
        prm = dict(mu_rkv=mu_rkv, mu_lora=mu_lora, w0=rw_w0[l][None],
                   w2=_bf(_pad_rows(rw_w2[l], LORA_PAD)), a0=rw_a0[l][None],
                   a2=_bf(_pad_rows(rw_a2[l], LORA_PAD)), k_k=rw_k_k[l][None], k_a=rw_k_a[l][None])
        if l > 0:
            prm.update(v0=rw_v0[l - 1][None], v1=_bf(_pad_cols(rw_v1[l - 1], LANES)),
                       v2=_bf(_pad_rows(rw_v2[l - 1], LANES)))
        ys = [[], [], []]
        for gi, grp in enumerate(groups):
            st_hg = state_hgrn[l] if grp.has_state else None
            st_rw = state_rwkv[l] if grp.has_state else None
            st_rn = state_ret[l] if grp.has_state else None
            st_rkv, st_lora = _pack_shift_cols(state_shift[l]) if grp.has_state else (None, None)

            y_hg, s_hg = _hgrn(z, grp, l, hg_lower_bounds, hg_norm[l][None], st_hg)
            seqs = _rw_prep(z, grp, prm, st_rkv, st_lora, v_first[gi])
            if l == 0:
                v_first[gi] = seqs[3]
            y_rw, s_rw = _rw_scan(seqs, grp, rw_ln_w[l][None], rw_ln_b[l][None],
                                  rw_r_k[l].reshape(1, BR_W), st_rw)
            y_rn, s_rn = _retention(z, grp, tables[gi], st_rn)
            last = z[grp.row0 + grp.seq - 1:grp.row0 + grp.rows:grp.seq]
            for lst, val in zip(new_states[gi], (s_hg, s_rw, _unpack_shift_rows(last), s_rn)):
                lst.append(val)
            for lst, val in zip(ys, (y_hg, y_rw, y_rn)):
                lst.append(val)
        ys = [jnp.concatenate(parts, axis=0) for parts in ys]
        m = _merge(ys, z, _bf(w_branch[l]))
        p = jnp.concatenate([p_prompt[l].reshape(N_PROMPT, PLE_DIM), p_sample[l].reshape(N_SAMPLE, PLE_DIM)], axis=0)
        x = _post(x, m, p, _bf(w_out[l]), norm_post[l][None], _bf(ple_up[l]), ple_norm[l][None], _bf(ple_gate[l]))

    y_prompt = x[:N_PROMPT].reshape(BATCH, SEQ, D_MODEL)
    y_sample = x[N_PROMPT:].reshape(DEC_BATCH, DEC_SEQ, D_MODEL)
    st = [[jnp.stack(per_layer, axis=0) for per_layer in grp_states] for grp_states in new_states]
    (hg_p, rw_p, sh_p, rn_p), (hg_s, rw_s, sh_s, rn_s) = st
    return (y_prompt, y_sample, hg_p, hg_s, rw_p, rw_s, sh_p, sh_s, rn_p, rn_s)
```

```python
import functools

import numpy as np
import jax
import jax.numpy as jnp
from jax import lax
from jax.experimental import pallas as pl
from jax.experimental.pallas import tpu as pltpu

F32 = jnp.float32
BF16 = jnp.bfloat16
HIGHEST = lax.Precision.HIGHEST

D_MODEL = 2048
BATCH, SEQ = 4, 2048
DEC_BATCH, DEC_SEQ = 128, 8
DEPTH = 2
PAST_LEN = 16384
N_BRANCH = 3
BR_W = 1024
HG_HEADS, HG_DK, HG_DV = 8, 128, 128
F_MIN = 1e-30
RW_HEADS, RW_N = 16, 64
RW_LORA = 96
RW_MV_LORA = 64
RW_GN_EPS = 64e-5
RN_HEADS, RN_DK, RN_DV = 4, 256, 256
ROPE_BASE = 10000.0
CHUNK = 64
PLE_DIM = 256
EPS = 1e-6

LANES = 128
SUBLANES = 8
MXU_DIM = 256

LORA_PAD = LANES
Z_HG = 0
Z_RKV = Z_HG + 3 * BR_W
Z_RN = Z_RKV + 3 * BR_W
Z_GATE = Z_RN + 3 * BR_W
Z_MERGE = Z_GATE + N_BRANCH * BR_W
Z_LORA = Z_MERGE + N_BRANCH * D_MODEL
Z_USED = Z_LORA + 2 * LORA_PAD
INPROJ_TN = 512
Z_W = -(-Z_USED // INPROJ_TN) * INPROJ_TN
INPROJ_TM = 1024

N_PROMPT = BATCH * SEQ
N_SAMPLE = DEC_BATCH * DEC_SEQ
N_ROWS = N_PROMPT + N_SAMPLE


def _bf(x):
    return x.astype(BF16)


def _dot(a, b):
    return jnp.dot(_bf(a), _bf(b), preferred_element_type=F32)


def _dot_nt(a, b):
    return lax.dot_general(_bf(a), _bf(b), (((1,), (1,)), ((), ())), preferred_element_type=F32)


def _dot_tn(a, b):
    return lax.dot_general(_bf(a), _bf(b), (((0,), (0,)), ((), ())), preferred_element_type=F32)


def _split_dot(x, g):
    hi = _bf(x)
    lo = _bf(x - hi.astype(F32))
    return (jnp.dot(hi, g, preferred_element_type=F32) + jnp.dot(lo, g, preferred_element_type=F32))


def _head_ones(width, head):
    r = lax.broadcasted_iota(jnp.int32, (width, width), 0) // head
    c = lax.broadcasted_iota(jnp.int32, (width, width), 1) // head
    return (r == c).astype(BF16)


def _rms(x):
    return x * lax.rsqrt(jnp.mean(x * x, axis=-1, keepdims=True) + EPS)


class _Group:
    def __init__(self, row0, batch, seq, pos0, has_state):
        self.row0, self.batch, self.seq, self.pos0, self.has_state = row0, batch, seq, pos0, has_state
        self.rows = batch * seq


PROMPT = _Group(0, BATCH, SEQ, 0, False)
SAMPLE = _Group(N_PROMPT, DEC_BATCH, DEC_SEQ, PAST_LEN, True)


def _inproj_kernel(x_ref, g_ref, w_ref, o_ref, h_ref):
    @pl.when(pl.program_id(1) == 0)
    def _():
        h_ref[...] = _bf(_rms(x_ref[...]) * g_ref[...])

    o_ref[...] = jnp.dot(h_ref[...], w_ref[...], preferred_element_type=F32)


def _inproj(x, g, w):
    n = x.shape[0]
    return pl.pallas_call(
        _inproj_kernel,
        grid=(n // INPROJ_TM, Z_W // INPROJ_TN),
        in_specs=[
            pl.BlockSpec((INPROJ_TM, D_MODEL), lambda i, j: (i, 0)),
            pl.BlockSpec((1, D_MODEL), lambda i, j: (0, 0)),
            pl.BlockSpec((D_MODEL, INPROJ_TN), lambda i, j: (0, j)),
        ],
        out_specs=pl.BlockSpec((INPROJ_TM, INPROJ_TN), lambda i, j: (i, j)),
        out_shape=jax.ShapeDtypeStruct((n, Z_W), F32),
        scratch_shapes=[pltpu.VMEM((INPROJ_TM, D_MODEL), BF16)],
        compiler_params=pltpu.CompilerParams(
            dimension_semantics=("parallel", "arbitrary"), vmem_limit_bytes=48 << 20),
        name="inproj",
    )(x, g, w)


def _pack_w_in(w):
    o_rw = 3 * BR_W
    r = w[:, o_rw:o_rw + BR_W]
    wlo = w[:, o_rw + BR_W:o_rw + BR_W + RW_LORA]
    k = w[:, o_rw + BR_W + RW_LORA:o_rw + 2 * BR_W + RW_LORA]
    v = w[:, o_rw + 2 * BR_W + RW_LORA:o_rw + 3 * BR_W + RW_LORA]
    alo = w[:, o_rw + 3 * BR_W + RW_LORA:o_rw + 3 * BR_W + 2 * RW_LORA]
    rest = w[:, o_rw + 3 * BR_W + 2 * RW_LORA:]
    zl = jnp.zeros((w.shape[0], LORA_PAD - RW_LORA), w.dtype)
    zt = jnp.zeros((w.shape[0], Z_W - Z_USED), w.dtype)
    return _bf(jnp.concatenate([w[:, :o_rw], r, k, v, rest, wlo, zl, alo, zl, zt], axis=1))


def _pack_shift_cols(a):
    r = a[..., :BR_W]
    wlo = a[..., BR_W:BR_W + RW_LORA]
    k = a[..., BR_W + RW_LORA:2 * BR_W + RW_LORA]
    v = a[..., 2 * BR_W + RW_LORA:3 * BR_W + RW_LORA]
    alo = a[..., 3 * BR_W + RW_LORA:]
    zl = jnp.zeros(a.shape[:-1] + (LORA_PAD - RW_LORA,), a.dtype)
    return jnp.concatenate([r, k, v], axis=-1), jnp.concatenate([wlo, zl, alo, zl], axis=-1)


def _unpack_shift_rows(zr):
    return jnp.concatenate([
        zr[:, Z_RKV:Z_RKV + BR_W],
        zr[:, Z_LORA:Z_LORA + RW_LORA],
        zr[:, Z_RKV + BR_W:Z_RKV + 3 * BR_W],
        zr[:, Z_LORA + LORA_PAD:Z_LORA + LORA_PAD + RW_LORA],
    ], axis=1)


def _gla_level_matrix(c):
    nlv = c.bit_length() - 1
    m = np.zeros(((nlv + 1) * c, c), np.float32)
    for r in range(c):
        m[r, :r + 1] = 1.0
        for p in range(nlv):
            bd = ((r >> (p + 1)) << (p + 1)) + (1 << p) - 1
            if (r >> p) & 1:
                m[(p + 1) * c + r, bd + 1:r + 1] = 1.0
            else:
                m[(p + 1) * c + r, r + 1:bd + 1] = 1.0
    return m


def _hgrn_kernel(*refs, c, nb, layer, has_state):
    if has_state:
        hq_ref, hf_ref, hi_ref, lbp_ref, gn_ref, ms_ref, s0_ref, y_ref, so_ref, s_scr = refs
    else:
        hq_ref, hf_ref, hi_ref, lbp_ref, gn_ref, ms_ref, y_ref, so_ref, s_scr = refs
    nlv = c.bit_length() - 1
    ci = pl.program_id(2)

    @pl.when(ci == 0)
    def _():
        if has_state:
            s_scr[...] = s0_ref[:, 0]
        else:
            s_scr[...] = jnp.zeros_like(s_scr)

    lbp = lbp_ref[...]
    e = jnp.exp(lbp - jnp.max(lbp, axis=0, keepdims=True))
    sm = e / jnp.sum(e, axis=0, keepdims=True)
    lb = jnp.sum(sm[0:layer + 1], axis=0, keepdims=True) - sm[0:1]

    ti = lax.broadcasted_iota(jnp.int32, (c, c), 0)
    si = lax.broadcasted_iota(jnp.int32, (c, c), 1)
    txs = ti ^ si
    lower = ti > si
    ones_c = jnp.ones((c, HG_DV), F32)

    for ib in range(nb):
        rows = pl.ds(ib * c, c)
        hq, hf, v = hq_ref[rows, :], hf_ref[rows, :], hi_ref[rows, :]
        sig = jax.nn.sigmoid(hf)
        f_gate = lb + (1.0 - lb) * sig
        logf = jnp.log(jnp.maximum(f_gate, F_MIN))
        k = (1.0 - lb) * (1.0 - sig)
        q = jax.nn.silu(hq) * HG_DK ** -0.5
        lv = jnp.dot(ms_ref[...], logf, precision=HIGHEST, preferred_element_type=F32)
        b = lv[0:c]
        att = jnp.where(ti == si, jnp.sum(q * k, axis=-1, keepdims=True), 0.0)
        for p in range(nlv):
            ex = jnp.exp(lv[(p + 1) * c:(p + 2) * c])
            att = att + jnp.where(((txs >> p) == 1) & lower, _dot_nt(q * ex, k * ex), 0.0)
        s = s_scr[ib]
        o = _dot(att, v) + _dot(q * jnp.exp(b), s)
        bl = b[c - 1:c]
        bl_col = lax.dot_general(logf, ones_c, (((0,), (0,)), ((), ())), precision=HIGHEST,
                                 preferred_element_type=F32)
        s_scr[ib] = jnp.exp(bl_col) * s + _dot_tn(k * jnp.exp(bl - b), v)
        y_ref[rows, :] = _rms(o) * gn_ref[...]

    @pl.when(ci == pl.num_programs(2) - 1)
    def _():
        so_ref[:, 0] = s_scr[...]


def _hgrn(z, grp, layer, lower_bounds, gnorm, state):
    c = min(CHUNK, grp.seq)
    nb = 1 if grp.seq > c else 8
    nchunk = grp.seq // c
    blk = nb * c
    base = grp.row0 // blk
    hb = HG_DK // LANES

    def zspec(col0):
        return pl.BlockSpec((blk, HG_DK), lambda i, h, ci: (base + i * nchunk + ci, col0 // HG_DK + h * hb))

    in_specs = [zspec(Z_HG), zspec(Z_HG + BR_W), zspec(Z_HG + 2 * BR_W),
                pl.BlockSpec((DEPTH, HG_DK), lambda i, h, ci: (0, h)),
                pl.BlockSpec((1, HG_DV), lambda i, h, ci: (0, 0)),
                pl.BlockSpec(((c.bit_length()) * c, c), lambda i, h, ci: (0, 0))]
    args = [z, z, z, lower_bounds, gnorm, jnp.asarray(_gla_level_matrix(c))]
    st_spec = pl.BlockSpec((nb, 1, HG_DK, HG_DV), lambda i, h, ci: (i, h, 0, 0))
    if grp.has_state:
        in_specs.append(st_spec)
        args.append(state)
    return pl.pallas_call(
        functools.partial(_hgrn_kernel, c=c, nb=nb, layer=layer, has_state=grp.has_state),
        grid=(grp.batch // nb, HG_HEADS, nchunk),
        in_specs=in_specs,
        out_specs=[pl.BlockSpec((blk, HG_DV), lambda i, h, ci: (i * nchunk + ci, h)), st_spec],
        out_shape=[jax.ShapeDtypeStruct((grp.rows, BR_W), F32),
                   jax.ShapeDtypeStruct((grp.batch, HG_HEADS, HG_DK, HG_DV), F32)],
        scratch_shapes=[pltpu.VMEM((nb, HG_DK, HG_DV), F32)],
        compiler_params=pltpu.CompilerParams(
            dimension_semantics=("parallel", "parallel", "arbitrary")),
        name="hgrn_" + ("sample" if grp.has_state else "prompt"),
    )(*args)


def _rope_kernel(inv_ref, cos_ref, sin_ref, *, pos0):
    t = cos_ref.shape[0]
    pos = (pos0 + lax.broadcasted_iota(jnp.int32, (t, RN_DK // 2), 0)).astype(F32)
    ang = pos * inv_ref[...]
    cos_ref[...] = jnp.cos(ang)
    sin_ref[...] = jnp.sin(ang)


def _rope_tables(grp):
    inv = ROPE_BASE ** (-jnp.arange(0, RN_DK, 2, dtype=F32) / RN_DK)
    shp = jax.ShapeDtypeStruct((grp.seq, RN_DK // 2), F32)
    return pl.pallas_call(functools.partial(_rope_kernel, pos0=grp.pos0), out_shape=[shp, shp],
                          name="rope")(inv[None, :])


def _ret_kernel(*refs, c, nb, has_state):
    if has_state:
        (q_ref, k_ref, v_ref, cos_ref, sin_ref, dm_ref, qd_ref, kd_ref, cd_ref, s0_ref,
         y_ref, so_ref, s_scr) = refs
    else:
        (q_ref, k_ref, v_ref, cos_ref, sin_ref, dm_ref, qd_ref, kd_ref, cd_ref,
         y_ref, so_ref, s_scr) = refs
    ci = pl.program_id(2)

    @pl.when(ci == 0)
    def _():
        if has_state:
            s_scr[...] = s0_ref[:, 0]
        else:
            s_scr[...] = jnp.zeros_like(s_scr)

    cos, sin = cos_ref[...], sin_ref[...]
    half = RN_DK // 2

    def rot(x):
        x1, x2 = x[:, :half], x[:, half:]
        return jnp.concatenate([x1 * cos - x2 * sin, x1 * sin + x2 * cos], axis=1)

    for ib in range(nb):
        rows = pl.ds(ib * c, c)
        q = rot(q_ref[rows, :])
        k = rot(k_ref[rows, :]) * RN_DK ** -0.5
        v = v_ref[rows, :]
        s = s_scr[ib]
        att = _dot_nt(q, k) * dm_ref[0]
        o = _dot(att, v) + _dot(q, s) * qd_ref[0]
        s_scr[ib] = cd_ref[0] * s + _dot_tn(k * kd_ref[0], v)
        y_ref[rows, :] = _rms(o)

    @pl.when(ci == pl.num_programs(2) - 1)
    def _():
        so_ref[:, 0] = s_scr[...]


def _retention(z, grp, tables, state):
    c = min(CHUNK, grp.seq)
    nb = 1 if grp.seq > c else 4
    nchunk = grp.seq // c
    blk = nb * c
    base = grp.row0 // blk
    log_gamma = jnp.log(1.0 - 2.0 ** (-5.0 - jnp.arange(RN_HEADS, dtype=F32)))
    j = jnp.arange(c, dtype=F32)
    rel = j[:, None] - j[None, :]
    dmat = jnp.where(rel >= 0, jnp.exp(log_gamma[:, None, None] * jnp.maximum(rel, 0.0)), 0.0)
    q_dec = jnp.exp(log_gamma[:, None] * (j + 1.0))[..., None]
    k_dec = jnp.exp(log_gamma[:, None] * (c - 1.0 - j))[..., None]
    c_dec = jnp.exp(log_gamma * c)[:, None, None]

    def zspec(col0):
        return pl.BlockSpec((blk, RN_DK), lambda i, h, ci: (base + i * nchunk + ci, col0 // RN_DK + h))

    tspec = pl.BlockSpec((c, RN_DK // 2), lambda i, h, ci: (ci, 0))
    in_specs = [zspec(Z_RN), zspec(Z_RN + BR_W), zspec(Z_RN + 2 * BR_W), tspec, tspec,
                pl.BlockSpec((1, c, c), lambda i, h, ci: (h, 0, 0)),
                pl.BlockSpec((1, c, 1), lambda i, h, ci: (h, 0, 0)),
                pl.BlockSpec((1, c, 1), lambda i, h, ci: (h, 0, 0)),
                pl.BlockSpec((1, 1, 1), lambda i, h, ci: (h, 0, 0))]
    args = [z, z, z, tables[0], tables[1], dmat, q_dec, k_dec, c_dec]
    st_spec = pl.BlockSpec((nb, 1, RN_DK, RN_DV), lambda i, h, ci: (i, h, 0, 0))
    if grp.has_state:
        in_specs.append(st_spec)
        args.append(state)
    return pl.pallas_call(
        functools.partial(_ret_kernel, c=c, nb=nb, has_state=grp.has_state),
        grid=(grp.batch // nb, RN_HEADS, nchunk),
        in_specs=in_specs,
        out_specs=[pl.BlockSpec((blk, RN_DV), lambda i, h, ci: (i * nchunk + ci, h)), st_spec],
        out_shape=[jax.ShapeDtypeStruct((grp.rows, BR_W), F32),
                   jax.ShapeDtypeStruct((grp.batch, RN_HEADS, RN_DK, RN_DV), F32)],
        scratch_shapes=[pltpu.VMEM((nb, RN_DK, RN_DV), F32)],
        compiler_params=pltpu.CompilerParams(
            dimension_semantics=("parallel", "parallel", "arbitrary")),
        name="ret_" + ("sample" if grp.has_state else "prompt"),
    )(*args)


RW_PREP_ROWS = 256


def _rw_prep_kernel(*refs, tt, nb, has_state, with_vmix):
    refs = list(refs)
    zr_ref, zl_ref = refs[:2]
    pos = 2
    if has_state:
        sr_ref, sl_ref = refs[pos:pos + 2]
        pos += 2
    (mur_ref, mul_ref, w0_ref, w2_ref, a0_ref, a2_ref, kk_ref, ka_ref) = refs[pos:pos + 8]
    pos += 8
    if with_vmix:
        v0_ref, v1_ref, v2_ref, vf_ref = refs[pos:pos + 4]
        pos += 4
    r_ref, d_ref, k_ref, v_ref, a_ref, b_ref = refs[pos:pos + 6]
    pr_scr, pl_scr = refs[pos + 6:pos + 8]
    ti = pl.program_id(1)
    rows = nb * tt

    zr, zl = zr_ref[...], zl_ref[...]
    first = (lax.broadcasted_iota(jnp.int32, (rows, 1), 0) % tt) == 0

    def prev_rows(z, st_ref, scr):
        w = z.shape[1]
        if nb > 1:
            st = st_ref[...]
            return jnp.broadcast_to(st[:, None, :], (nb, tt, w)).reshape(rows, w)

        @pl.when(ti == 0)
        def _():
            scr[0:1, :] = st_ref[...] if has_state else jnp.zeros((1, w), F32)

        return scr[0:1, :]

    def shifted(z, st_ref, scr, mu):
        prev = jnp.where(first, prev_rows(z, st_ref, scr), pltpu.roll(z, 1, 0))
        scr[0:1, :] = z[rows - 1:rows, :]
        return z + (prev - z) * mu

    ur = shifted(zr, sr_ref if has_state else None, pr_scr, mur_ref[...])
    ul = shifted(zl, sl_ref if has_state else None, pl_scr, mul_ref[...])

    r, k, v = ur[:, :BR_W], ur[:, BR_W:2 * BR_W], ur[:, 2 * BR_W:]
    wlo, alo = ul[:, :LORA_PAD], ul[:, LORA_PAD:]
    w = -jax.nn.softplus(-(w0_ref[...] + _dot(jnp.tanh(wlo), w2_ref[...]))) - 0.5
    a = jax.nn.sigmoid(a0_ref[...] + _dot(alo, a2_ref[...]))
    kk = k * kk_ref[...]
    ones = _head_ones(MXU_DIM, RW_N)
    kk2 = kk * kk
    nrm2 = jnp.concatenate(
        [_split_dot(kk2[:, g * MXU_DIM:(g + 1) * MXU_DIM], ones) for g in range(BR_W // MXU_DIM)], axis=1)
    kk = kk / jnp.maximum(jnp.sqrt(nrm2), 1e-12)
    k = k * (1.0 + (a - 1.0) * ka_ref[...])
    if with_vmix:
        mix = jax.nn.sigmoid(v0_ref[...] + _dot(_dot(v, v1_ref[...]), v2_ref[...]))
        v = v + (vf_ref[...] - v) * mix
    r_ref[...] = r
    d_ref[...] = -jnp.exp(w)
    k_ref[...] = k
    v_ref[...] = v
    a_ref[...] = -kk
    b_ref[...] = kk * a


def _rw_prep(z, grp, prm, state_rkv, state_lora, v_first):
    tt = min(RW_PREP_ROWS, grp.seq)
    nb = RW_PREP_ROWS // tt
    ntile = grp.seq // tt
    rows = nb * tt
    base = grp.row0 // rows
    with_vmix = v_first is not None
    w_rkv, w_lora = 3 * BR_W, 2 * LORA_PAD

    def rowmap(i, t):
        return base + i * ntile + t

    def const(shape):
        return pl.BlockSpec(shape, lambda i, t: (0,) * len(shape))

    in_specs = [pl.BlockSpec((rows, w_rkv), lambda i, t: (rowmap(i, t), Z_RKV // w_rkv)),
                pl.BlockSpec((rows, w_lora), lambda i, t: (rowmap(i, t), Z_LORA // w_lora))]
    args = [z, z]
    if grp.has_state:
        in_specs += [pl.BlockSpec((nb, w_rkv), lambda i, t: (i, 0)),
                     pl.BlockSpec((nb, w_lora), lambda i, t: (i, 0))]
        args += [state_rkv, state_lora]
    in_specs += [const((1, w_rkv)), const((1, w_lora)), const((1, BR_W)), const((LORA_PAD, BR_W)),
                 const((1, BR_W)), const((LORA_PAD, BR_W)), const((1, BR_W)), const((1, BR_W))]
    args += [prm['mu_rkv'], prm['mu_lora'], prm['w0'], prm['w2'], prm['a0'], prm['a2'], prm['k_k'], prm['k_a']]
    out_spec = pl.BlockSpec((rows, BR_W), lambda i, t: (i * ntile + t, 0))
    if with_vmix:
        in_specs += [const((1, BR_W)), const((BR_W, LANES)), const((LANES, BR_W)), out_spec]
        args += [prm['v0'], prm['v1'], prm['v2'], v_first]
    out = jax.ShapeDtypeStruct((grp.rows, BR_W), F32)
    return pl.pallas_call(
        functools.partial(_rw_prep_kernel, tt=tt, nb=nb, has_state=grp.has_state, with_vmix=with_vmix),
        grid=(grp.batch // nb, ntile),
        in_specs=in_specs,
        out_specs=[out_spec] * 6,
        out_shape=[out] * 6,
        scratch_shapes=[pltpu.VMEM((SUBLANES, w_rkv), F32), pltpu.VMEM((SUBLANES, w_lora), F32)],
        compiler_params=pltpu.CompilerParams(
            dimension_semantics=("parallel", "arbitrary"), vmem_limit_bytes=48 << 20),
        name="rw_prep_" + ("sample" if grp.has_state else "prompt"),
    )(*args)


RW_TILE = MXU_DIM
RW_TILE_HEADS = RW_TILE // RW_N
RW_SCAN_NB = 4
RW_SCAN_NG = 2
RW_SCAN_TT = 128


def _rw_scan_kernel(*refs, tt, nb, ng, has_state):
    if has_state:
        (r_ref, d_ref, k_ref, v_ref, a_ref, b_ref, lw_ref, lb_ref, rk_ref, s0_ref,
         y_ref, so_ref, s_scr, y_scr) = refs
    else:
        (r_ref, d_ref, k_ref, v_ref, a_ref, b_ref, lw_ref, lb_ref, rk_ref,
         y_ref, so_ref, s_scr, y_scr) = refs
    ti = pl.program_id(2)
    chains = [(ib, g) for ib in range(nb) for g in range(ng)]

    @pl.when(ti == 0)
    def _():
        for ib, g in chains:
            if has_state:
                s_scr[ib, g] = jnp.concatenate(
                    [s0_ref[ib, g * RW_TILE_HEADS + h] for h in range(RW_TILE_HEADS)], axis=1)
            else:
                s_scr[ib, g] = jnp.zeros((RW_N, RW_TILE), F32)

    ones = _head_ones(RW_TILE, RW_N)
    eye = (lax.broadcasted_iota(jnp.int32, (RW_N, RW_TILE), 0)
           == lax.broadcasted_iota(jnp.int32, (RW_N, RW_TILE), 1) % RW_N).astype(F32)

    def row(ref, ib, g, t):
        return ref[ib, pl.ds(t, 1), g * RW_TILE:(g + 1) * RW_TILE]

    def step(t, carry):
        lhs = []
        for ib, g in chains:
            p = s_scr[ib, g] * row(a_ref, ib, g, t)
            hi = _bf(p)
            lhs += [hi, _bf(p - hi.astype(F32)), _bf(eye * row(v_ref, ib, g, t))]
        res = jnp.dot(jnp.concatenate(lhs, axis=0), ones, preferred_element_type=F32)
        lhs = []
        for n, (ib, g) in enumerate(chains):
            o = 3 * n * RW_N
            sa = res[o:o + RW_N] + res[o + RW_N:o + 2 * RW_N]
            vcol = res[o + 2 * RW_N:o + 3 * RW_N]
            s = (s_scr[ib, g] * jnp.exp(row(d_ref, ib, g, t)) + sa * row(b_ref, ib, g, t)
                 + vcol * row(k_ref, ib, g, t))
            s_scr[ib, g] = s
            lhs.append(_bf(s * row(r_ref, ib, g, t)))
        res = jnp.dot(jnp.concatenate(lhs, axis=0), ones, preferred_element_type=F32)
        for n, (ib, g) in enumerate(chains):
            yb = res[n * RW_N:(n + 1) * RW_N]
            y_scr[ib, pl.ds(t, 1), g * RW_TILE:(g + 1) * RW_TILE] = jnp.sum(yb * eye, axis=0, keepdims=True)
        return carry

    lax.fori_loop(0, tt, step, 0)

    for ib, g in chains:
        sl = slice(g * RW_TILE, (g + 1) * RW_TILE)
        y = y_scr[ib, :, sl]
        mu = _split_dot(y, ones) * (1.0 / RW_N)
        dlt = y - mu
        var = _split_dot(dlt * dlt, ones) * (1.0 / RW_N)
        yn = dlt * lax.rsqrt(var + RW_GN_EPS) * lw_ref[:, sl] + lb_ref[:, sl]
        bonus = _split_dot(r_ref[ib, :, sl] * k_ref[ib, :, sl] * rk_ref[:, sl], ones)
        y_ref[ib, :, sl] = yn + bonus * v_ref[ib, :, sl]

    @pl.when(ti == pl.num_programs(2) - 1)
    def _():
        for ib, g in chains:
            s = s_scr[ib, g]
            for h in range(RW_TILE_HEADS):
                so_ref[ib, g * RW_TILE_HEADS + h] = s[:, h * RW_N:(h + 1) * RW_N]


def _rw_scan(seqs, grp, ln_w, ln_b, r_k, state):
    tt = min(RW_SCAN_TT, grp.seq)
    nb, ng = RW_SCAN_NB, RW_SCAN_NG
    width = ng * RW_TILE
    nh = ng * RW_TILE_HEADS
    seq_spec = pl.BlockSpec((nb, tt, width), lambda i, j, t: (i, t, j))
    par_spec = pl.BlockSpec((1, width), lambda i, j, t: (0, j))
    st_spec = pl.BlockSpec((nb, nh, RW_N, RW_N), lambda i, j, t: (i, j, 0, 0))
    in_specs = [seq_spec] * 6 + [par_spec] * 3
    args = [s.reshape(grp.batch, grp.seq, BR_W) for s in seqs] + [ln_w, ln_b, r_k]
    if grp.has_state:
        in_specs.append(st_spec)
        args.append(state)
    y, s_out = pl.pallas_call(
        functools.partial(_rw_scan_kernel, tt=tt, nb=nb, ng=ng, has_state=grp.has_state),
        grid=(grp.batch // nb, BR_W // width, grp.seq // tt),
        in_specs=in_specs,
        out_specs=[seq_spec, st_spec],
        out_shape=[jax.ShapeDtypeStruct((grp.batch, grp.seq, BR_W), F32),
                   jax.ShapeDtypeStruct((grp.batch, RW_HEADS, RW_N, RW_N), F32)],
        scratch_shapes=[pltpu.VMEM((nb, ng, RW_N, RW_TILE), F32), pltpu.VMEM((nb, tt, width), F32)],
        compiler_params=pltpu.CompilerParams(
            dimension_semantics=("parallel", "parallel", "arbitrary"), vmem_limit_bytes=48 << 20),
        name="rw_scan_" + ("sample" if grp.has_state else "prompt"),
    )(*args)
    return y.reshape(grp.rows, BR_W), s_out


RW_CHUNK = RW_N


def _rw_chunk_kernel(r_ref, d_ref, k_ref, v_ref, a_ref, b_ref, lw_ref, lb_ref, rk_ref, y_ref, ho_ref, h_scr):
    ci = pl.program_id(1)
    ln, tw, nt = RW_CHUNK, RW_TILE, BR_W // RW_TILE

    @pl.when(ci == 0)
    def _():
        h_scr[...] = jnp.zeros_like(h_scr)

    row = lax.broadcasted_iota(jnp.int32, (tw, tw), 0)
    col = lax.broadcasted_iota(jnp.int32, (tw, tw), 1)
    same_head = (row // RW_N) == (col // RW_N)
    ones = same_head.astype(BF16)
    t_i = lax.broadcasted_iota(jnp.int32, (ln, tw), 0)
    s_i = lax.broadcasted_iota(jnp.int32, (ln, tw), 1) % ln
    strict, incl = s_i < t_i, s_i <= t_i
    tril = (lax.broadcasted_iota(jnp.int32, (ln, ln), 0) >= lax.broadcasted_iota(jnp.int32, (ln, ln), 1)).astype(F32)

    def blocks(x):
        return jnp.where(same_head, jnp.concatenate([x] * RW_TILE_HEADS, axis=0), 0.0).astype(BF16)

    def mm(x, w):
        return jnp.dot(_bf(x), w, preferred_element_type=F32)

    ld_all = d_ref[...]
    c_all = jnp.dot(tril, ld_all, precision=HIGHEST, preferred_element_type=F32)
    tot_col = lax.dot_general(ld_all, jnp.ones((ln, LANES), F32), (((0,), (0,)), ((), ())),
                              precision=HIGHEST, preferred_element_type=F32)

    for g in range(nt):
        sl = slice(g * tw, (g + 1) * tw)
        r, ld, k, v, a, b = (ref[:, sl] for ref in (r_ref, d_ref, k_ref, v_ref, a_ref, b_ref))
        c = c_all[:, sl]
        c_last = c[ln - 1:ln]
        e_inv, e_end = jnp.exp(-c), jnp.exp(c_last - c)
        at, rt = a * jnp.exp(c - ld), r * jnp.exp(c)
        bt, kt, bp, kp = b * e_inv, k * e_inv, b * e_end, k * e_end
        gram = lax.dot_general(_bf(jnp.concatenate([at, rt], axis=0)),
                               jnp.concatenate([blocks(bt), blocks(kt)], axis=0),
                               (((1,), (1,)), ((), ())), preferred_element_type=F32)
        n_ab = jnp.where(strict, gram[:ln, :tw], 0.0)
        a_ak = jnp.where(strict, gram[:ln, tw:], 0.0)
        a_rb = jnp.where(incl, gram[ln:, :tw], 0.0)
        a_rk = jnp.where(incl, gram[ln:, tw:], 0.0)
        v_blk = blocks(v)
        av = mm(jnp.concatenate([a_ak, a_rk], axis=0), v_blk)
        w = jnp.concatenate([at, av[:ln]], axis=1)
        p = n_ab
        for lvl in range(ln.bit_length() - 1):
            w = w + mm(p, jnp.concatenate([blocks(w[:, :tw]), blocks(w[:, tw:])], axis=1))
            if lvl < ln.bit_length() - 2:
                p = mm(p, blocks(p))
        at2, uv = w[:, :tw], w[:, tw:]
        m_off = jnp.where(same_head, _dot_tn(bp, at2), 0.0)
        cc = jnp.where(same_head, _dot_tn(jnp.concatenate([bp, kp], axis=0), jnp.concatenate([uv, v], axis=0)), 0.0)
        qy = mm(a_rb, jnp.concatenate([blocks(at2), blocks(uv)], axis=1))
        q = rt + qy[:, :tw]
        yc = qy[:, tw:] + av[ln:]
        h = h_scr[g]
        hb = _bf(h)
        y = mm(q, hb) + yc
        decay_col = jnp.exp(tot_col[g * tw:(g + 1) * tw])
        h_scr[g] = jnp.concatenate([decay_col] * (tw // LANES), axis=1) * h + mm(m_off, hb) + cc

        mu = _split_dot(y, ones) * (1.0 / RW_N)
        dlt = y - mu
        var = _split_dot(dlt * dlt, ones) * (1.0 / RW_N)
        yn = dlt * lax.rsqrt(var + RW_GN_EPS) * lw_ref[:, sl] + lb_ref[:, sl]
        y_ref[:, sl] = yn + _split_dot(r * k * rk_ref[:, sl], ones) * v

    @pl.when(ci == pl.num_programs(1) - 1)
    def _():
        for g in range(nt):
            h = h_scr[g]
            for hd in range(RW_TILE_HEADS):
                ho_ref[0, g * RW_TILE_HEADS + hd] = h[hd * RW_N:(hd + 1) * RW_N, hd * RW_N:(hd + 1) * RW_N]


def _rw_scan_chunked(seqs, grp, ln_w, ln_b, r_k):
    assert not grp.has_state and grp.seq % RW_CHUNK == 0
    nchunk = grp.seq // RW_CHUNK
    seq_spec = pl.BlockSpec((RW_CHUNK, BR_W), lambda i, c: (i * nchunk + c, 0))
    par_spec = pl.BlockSpec((1, BR_W), lambda i, c: (0, 0))
    y, h_out = pl.pallas_call(
        _rw_chunk_kernel,
        grid=(grp.batch, nchunk),
        in_specs=[seq_spec] * 6 + [par_spec] * 3,
        out_specs=[seq_spec, pl.BlockSpec((1, RW_HEADS, RW_N, RW_N), lambda i, c: (i, 0, 0, 0))],
        out_shape=[jax.ShapeDtypeStruct((grp.rows, BR_W), F32),
                   jax.ShapeDtypeStruct((grp.batch, RW_HEADS, RW_N, RW_N), F32)],
        scratch_shapes=[pltpu.VMEM((BR_W // RW_TILE, RW_TILE, RW_TILE), F32)],
        compiler_params=pltpu.CompilerParams(
            dimension_semantics=("parallel", "arbitrary"), vmem_limit_bytes=48 << 20),
        name="rw_chunk_prompt",
    )(*seqs, ln_w, ln_b, r_k)
    return y, jnp.swapaxes(h_out, -1, -2)


MERGE_TM = 256


def _merge_kernel(y0_ref, y1_ref, y2_ref, g_ref, zm_ref, wb_ref, m_ref):
    acc = None
    for n, y_ref in enumerate((y0_ref, y1_ref, y2_ref)):
        ys = y_ref[...] * jax.nn.silu(g_ref[:, n * BR_W:(n + 1) * BR_W])
        proj = jnp.dot(_bf(ys), wb_ref[n], preferred_element_type=F32)
        term = jax.nn.sigmoid(zm_ref[:, n * D_MODEL:(n + 1) * D_MODEL]) * proj
        acc = term if acc is None else acc + term
    m_ref[...] = _bf(acc)


def _merge(ys, z, wb):
    n = z.shape[0]
    yspec = pl.BlockSpec((MERGE_TM, BR_W), lambda i: (i, 0))
    gw, mw = N_BRANCH * BR_W, N_BRANCH * D_MODEL
    return pl.pallas_call(
        _merge_kernel,
        grid=(n // MERGE_TM,),
        in_specs=[yspec, yspec, yspec,
                  pl.BlockSpec((MERGE_TM, gw), lambda i: (i, Z_GATE // gw)),
                  pl.BlockSpec((MERGE_TM, mw), lambda i: (i, Z_MERGE // mw)),
                  pl.BlockSpec((N_BRANCH, BR_W, D_MODEL), lambda i: (0, 0, 0), pipeline_mode=pl.Buffered(1))],
        out_specs=pl.BlockSpec((MERGE_TM, D_MODEL), lambda i: (i, 0)),
        out_shape=jax.ShapeDtypeStruct((n, D_MODEL), BF16),
        compiler_params=pltpu.CompilerParams(
            dimension_semantics=("parallel",), vmem_limit_bytes=48 << 20),
        name="merge",
    )(*ys, z, z, wb)


POST_TM = 256


def _post_kernel(x_ref, m_ref, p_ref, wo_ref, np_ref, pu_ref, pn_ref, pg_ref, o_ref):
    x = x_ref[...] + _rms(jnp.dot(m_ref[...], wo_ref[...], preferred_element_type=F32)) * np_ref[...]
    e = _rms(jnp.dot(_bf(p_ref[...]), pu_ref[...], preferred_element_type=F32)) * pn_ref[...]
    o_ref[...] = x + e * jax.nn.sigmoid(jnp.dot(_bf(x), pg_ref[...], preferred_element_type=F32))


def _post(x, m, p, wo, norm_post, ple_up, ple_norm, ple_gate):
    n = x.shape[0]

    def rows(w):
        return pl.BlockSpec((POST_TM, w), lambda i: (i, 0))

    def const(shape):
        return pl.BlockSpec(shape, lambda i: (0, 0), pipeline_mode=pl.Buffered(1))

    return pl.pallas_call(
        _post_kernel,
        grid=(n // POST_TM,),
        in_specs=[rows(D_MODEL), rows(D_MODEL), rows(PLE_DIM), const((D_MODEL, D_MODEL)), const((1, D_MODEL)),
                  const((PLE_DIM, D_MODEL)), const((1, D_MODEL)), const((D_MODEL, D_MODEL))],
        out_specs=rows(D_MODEL),
        out_shape=jax.ShapeDtypeStruct((n, D_MODEL), F32),
        compiler_params=pltpu.CompilerParams(
            dimension_semantics=("parallel",), vmem_limit_bytes=48 << 20),
        name="post",
    )(x, m, p, wo, norm_post, ple_up, ple_norm, ple_gate)


def _pad_rows(a, rows):
    return jnp.concatenate([a, jnp.zeros((rows - a.shape[0],) + a.shape[1:], a.dtype)], axis=0)


def _pad_cols(a, cols):
    return jnp.concatenate([a, jnp.zeros(a.shape[:-1] + (cols - a.shape[-1],), a.dtype)], axis=-1)


def kernel(x_prompt, x_sample, state_hgrn, state_rwkv, state_shift, state_ret, p_prompt, p_sample,
           norm_pre, w_in, hg_lower_bounds, hg_norm, rw_mu, rw_w0, rw_w2, rw_a0, rw_a2, rw_k_k, rw_k_a,
           rw_v0, rw_v1, rw_v2, rw_r_k, rw_ln_w, rw_ln_b, w_branch, w_out, norm_post, ple_up, ple_norm, ple_gate):
    groups = (PROMPT, SAMPLE)
    x = jnp.concatenate([x_prompt.reshape(N_PROMPT, D_MODEL), x_sample.reshape(N_SAMPLE, D_MODEL)], axis=0)
    tables = [_rope_tables(g) for g in groups]
    v_first = [None, None]
    new_states = [[[], [], [], []] for _ in groups]

    for l in range(DEPTH):
        z = _inproj(x, norm_pre[l][None], _pack_w_in(w_in[l]))
        mu_rkv, mu_lora = _pack_shift_cols(rw_mu[l][None])
        prm = dict(mu_rkv=mu_rkv, mu_lora=mu_lora, w0=rw_w0[l][None],
                   w2=_bf(_pad_rows(rw_w2[l], LORA_PAD)), a0=rw_a0[l][None],
                   a2=_bf(_pad_rows(rw_a2[l], LORA_PAD)), k_k=rw_k_k[l][None], k_a=rw_k_a[l][None])
        if l > 0:
            prm.update(v0=rw_v0[l - 1][None], v1=_bf(_pad_cols(rw_v1[l - 1], LANES)),
                       v2=_bf(_pad_rows(rw_v2[l - 1], LANES)))
        ys = [[], [], []]
        for gi, grp in enumerate(groups):
            st_hg = state_hgrn[l] if grp.has_state else None
            st_rw = state_rwkv[l] if grp.has_state else None
            st_rn = state_ret[l] if grp.has_state else None
            st_rkv, st_lora = _pack_shift_cols(state_shift[l]) if grp.has_state else (None, None)

            y_hg, s_hg = _hgrn(z, grp, l, hg_lower_bounds, hg_norm[l][None], st_hg)
            seqs = _rw_prep(z, grp, prm, st_rkv, st_lora, v_first[gi])
            if l == 0:
                v_first[gi] = seqs[3]
            rw_norm = (rw_ln_w[l][None], rw_ln_b[l][None], rw_r_k[l].reshape(1, BR_W))
            if grp.has_state or grp.seq % RW_CHUNK:
                y_rw, s_rw = _rw_scan(seqs, grp, *rw_norm, st_rw)
            else:
                y_rw, s_rw = _rw_scan_chunked(seqs, grp, *rw_norm)
            y_rn, s_rn = _retention(z, grp, tables[gi], st_rn)
            last = z[grp.row0 + grp.seq - 1:grp.row0 + grp.rows:grp.seq]
            for lst, val in zip(new_states[gi], (s_hg, s_rw, _unpack_shift_rows(last), s_rn)):
                lst.append(val)
            for lst, val in zip(ys, (y_hg, y_rw, y_rn)):
                lst.append(val)
        ys = [jnp.concatenate(parts, axis=0) for parts in ys]
        m = _merge(ys, z, _bf(w_branch[l]))
        p = jnp.concatenate([p_prompt[l].reshape(N_PROMPT, PLE_DIM), p_sample[l].reshape(N_SAMPLE, PLE_DIM)], axis=0)
        x = _post(x, m, p, _bf(w_out[l]), norm_post[l][None], _bf(ple_up[l]), ple_norm[l][None], _bf(ple_gate[l]))

    y_prompt = x[:N_PROMPT].reshape(BATCH, SEQ, D_MODEL)
    y_sample = x[N_PROMPT:].reshape(DEC_BATCH, DEC_SEQ, D_MODEL)
    st = [[jnp.stack(per_layer, axis=0) for per_layer in grp_states] for grp_states in new_states]
    (hg_p, rw_p, sh_p, rn_p), (hg_s, rw_s, sh_s, rn_s) = st
    return (y_prompt, y_sample, hg_p, hg_s, rw_p, rw_s, sh_p, sh_s, rn_p, rn_s)
```

```python
import functools

import numpy as np
import jax
import jax.numpy as jnp
from jax import lax
from jax.experimental import pallas as pl
from jax.experimental.pallas import tpu as pltpu

F32 = jnp.float32
BF16 = jnp.bfloat16
HIGHEST = lax.Precision.HIGHEST

D_MODEL = 2048
BATCH, SEQ = 4, 2048
DEC_BATCH, DEC_SEQ = 128, 8
DEPTH = 2
PAST_LEN = 16384
N_BRANCH = 3
BR_W = 1024
HG_HEADS, HG_DK, HG_DV = 8, 128, 128
F_MIN = 1e-30
RW_HEADS, RW_N = 16, 64
RW_LORA = 96
RW_MV_LORA = 64
RW_GN_EPS = 64e-5
RN_HEADS, RN_DK, RN_DV = 4, 256, 256
ROPE_BASE = 10000.0
CHUNK = 64
PLE_DIM = 256
EPS = 1e-6

LANES = 128
SUBLANES = 8
MXU_DIM = 256

LORA_PAD = LANES
Z_HG = 0
Z_RKV = Z_HG + 3 * BR_W
Z_RN = Z_RKV + 3 * BR_W
Z_GATE = Z_RN + 3 * BR_W
Z_MERGE = Z_GATE + N_BRANCH * BR_W
Z_LORA = Z_MERGE + N_BRANCH * D_MODEL
Z_USED = Z_LORA + 2 * LORA_PAD
INPROJ_TN = 512
Z_W = -(-Z_USED // INPROJ_TN) * INPROJ_TN
INPROJ_TM = 1024

N_PROMPT = BATCH * SEQ
N_SAMPLE = DEC_BATCH * DEC_SEQ
N_ROWS = N_PROMPT + N_SAMPLE


def _bf(x):
    return x.astype(BF16)


def _dot(a, b):
    return jnp.dot(_bf(a), _bf(b), preferred_element_type=F32)


def _dot_nt(a, b):
    return lax.dot_general(_bf(a), _bf(b), (((1,), (1,)), ((), ())), preferred_element_type=F32)


def _dot_tn(a, b):
    return lax.dot_general(_bf(a), _bf(b), (((0,), (0,)), ((), ())), preferred_element_type=F32)


def _split_dot(x, g):
    hi = _bf(x)
    lo = _bf(x - hi.astype(F32))
    return (jnp.dot(hi, g, preferred_element_type=F32) + jnp.dot(lo, g, preferred_element_type=F32))


def _head_ones(width, head):
    r = lax.broadcasted_iota(jnp.int32, (width, width), 0) // head
    c = lax.broadcasted_iota(jnp.int32, (width, width), 1) // head
    return (r == c).astype(BF16)


def _rms(x):
    return x * lax.rsqrt(jnp.mean(x * x, axis=-1, keepdims=True) + EPS)


def _round_robin(gens):
    live = list(gens)
    while live:
        nxt = []
        for gen in live:
            try:
                next(gen)
                nxt.append(gen)
            except StopIteration:
                pass
        live = nxt


class _Group:
    def __init__(self, name, row0, batch, seq, pos0, has_state):
        self.name, self.row0, self.batch, self.seq, self.pos0, self.has_state = name, row0, batch, seq, pos0, has_state
        self.rows = batch * seq


PROMPT = _Group("prompt", 0, BATCH, SEQ, 0, False)
SAMPLE = _Group("sample", N_PROMPT, DEC_BATCH, DEC_SEQ, PAST_LEN, True)


def _pcall(body, *, name, grid, in_specs, args, out_specs, out_shape, into=None, scratch=(), semantics,
           vmem_mb=None):
    in_specs, args = list(in_specs), list(args)
    n_in, aliases = len(args), {}
    for k, arr in enumerate(into or ()):
        if arr is not None:
            aliases[len(args)] = k
            in_specs.append(pl.BlockSpec(memory_space=pl.ANY))
            args.append(arr)
    n_alias = len(aliases)

    def kernel_fn(*refs):
        return body(*refs[:n_in], *refs[n_in + n_alias:])

    params = dict(dimension_semantics=semantics)
    if vmem_mb is not None:
        params['vmem_limit_bytes'] = vmem_mb << 20
    return pl.pallas_call(
        kernel_fn, grid=grid, in_specs=in_specs, out_specs=out_specs, out_shape=out_shape,
        scratch_shapes=list(scratch), input_output_aliases=aliases,
        compiler_params=pltpu.CompilerParams(**params), name=name)(*args)


def _layer_block(layer, shape, index_map):
    return pl.BlockSpec((None,) + tuple(shape), lambda *g: (layer,) + tuple(index_map(*g)))


D_IN = 3 * BR_W + (3 * BR_W + 2 * RW_LORA) + 3 * BR_W + N_BRANCH * BR_W + N_BRANCH * D_MODEL
PACK_ROWS = 128
PACK_COLS = 2048


def _pack_kernel(w_ref, o_ref):
    o_rw = 3 * BR_W

    def copy(dst, src, n):
        for c in range(0, n, PACK_COLS):
            m = min(PACK_COLS, n - c)
            o_ref[:, dst + c:dst + c + m] = _bf(w_ref[:, src + c:src + c + m])

    copy(Z_HG, 0, o_rw + BR_W)
    copy(Z_RKV + BR_W, o_rw + BR_W + RW_LORA, 2 * BR_W)
    rest = o_rw + 3 * BR_W + 2 * RW_LORA
    copy(Z_RN, rest, D_IN - rest)
    zl = jnp.zeros((PACK_ROWS, LORA_PAD - RW_LORA), BF16)
    wlo = _bf(w_ref[:, o_rw + BR_W:o_rw + BR_W + RW_LORA])
    alo = _bf(w_ref[:, rest - RW_LORA:rest])
    o_ref[:, Z_LORA:Z_USED] = jnp.concatenate([wlo, zl, alo, zl], axis=1)
    o_ref[:, Z_USED:] = jnp.zeros((PACK_ROWS, Z_W - Z_USED), BF16)


def _pack_w_in(w_in):
    return _pcall(
        _pack_kernel, name="pack_w_in", grid=(DEPTH, D_MODEL // PACK_ROWS),
        in_specs=[pl.BlockSpec((None, PACK_ROWS, D_IN), lambda l, i: (l, i, 0))], args=[w_in],
        out_specs=pl.BlockSpec((None, PACK_ROWS, Z_W), lambda l, i: (l, i, 0)),
        out_shape=jax.ShapeDtypeStruct((DEPTH, D_MODEL, Z_W), BF16),
        semantics=("parallel", "parallel"), vmem_mb=48)


def _pack_shift_cols(a):
    r = a[..., :BR_W]
    wlo = a[..., BR_W:BR_W + RW_LORA]
    k = a[..., BR_W + RW_LORA:2 * BR_W + RW_LORA]
    v = a[..., 2 * BR_W + RW_LORA:3 * BR_W + RW_LORA]
    alo = a[..., 3 * BR_W + RW_LORA:]
    zl = jnp.zeros(a.shape[:-1] + (LORA_PAD - RW_LORA,), a.dtype)
    return jnp.concatenate([r, k, v], axis=-1), jnp.concatenate([wlo, zl, alo, zl], axis=-1)


def _unpack_shift_rows(zr):
    return jnp.concatenate([
        zr[:, Z_RKV:Z_RKV + BR_W],
        zr[:, Z_LORA:Z_LORA + RW_LORA],
        zr[:, Z_RKV + BR_W:Z_RKV + 3 * BR_W],
        zr[:, Z_LORA + LORA_PAD:Z_LORA + LORA_PAD + RW_LORA],
    ], axis=1)


def _inproj_kernel(x_ref, g_ref, w_ref, o_ref, h_ref):
    @pl.when(pl.program_id(1) == 0)
    def _():
        h_ref[...] = _bf(_rms(x_ref[...]) * g_ref[...])

    o_ref[...] = jnp.dot(h_ref[...], w_ref[...], preferred_element_type=F32)


def _inproj(x, g, w, layer):
    n = x.shape[0]
    return _pcall(
        _inproj_kernel, name="inproj", grid=(n // INPROJ_TM, Z_W // INPROJ_TN),
        in_specs=[pl.BlockSpec((INPROJ_TM, D_MODEL), lambda i, j: (i, 0)),
                  _layer_block(layer, (1, D_MODEL), lambda i, j: (0, 0)),
                  _layer_block(layer, (D_MODEL, INPROJ_TN), lambda i, j: (0, j))],
        args=[x, g, w],
        out_specs=pl.BlockSpec((INPROJ_TM, INPROJ_TN), lambda i, j: (i, j)),
        out_shape=jax.ShapeDtypeStruct((n, Z_W), F32),
        scratch=[pltpu.VMEM((INPROJ_TM, D_MODEL), BF16)],
        semantics=("parallel", "arbitrary"), vmem_mb=48)


def _gla_level_matrix(c):
    nlv = c.bit_length() - 1
    m = np.zeros(((nlv + 1) * c, c), np.float32)
    for r in range(c):
        m[r, :r + 1] = 1.0
        for p in range(nlv):
            bd = ((r >> (p + 1)) << (p + 1)) + (1 << p) - 1
            if (r >> p) & 1:
                m[(p + 1) * c + r, bd + 1:r + 1] = 1.0
            else:
                m[(p + 1) * c + r, r + 1:bd + 1] = 1.0
    return m


def _hgrn_kernel(*refs, c, nb, layer, has_state):
    if has_state:
        hq_ref, hf_ref, hi_ref, lbp_ref, gn_ref, ms_ref, s0_ref, y_ref, so_ref, s_scr = refs
    else:
        hq_ref, hf_ref, hi_ref, lbp_ref, gn_ref, ms_ref, y_ref, so_ref, s_scr = refs
    nlv = c.bit_length() - 1
    ci = pl.program_id(1)

    @pl.when(ci == 0)
    def _():
        if has_state:
            s_scr[...] = s0_ref[...]
        else:
            s_scr[...] = jnp.zeros_like(s_scr)

    lbp = lbp_ref[...]
    e = jnp.exp(lbp - jnp.max(lbp, axis=0, keepdims=True))
    sm = e / jnp.sum(e, axis=0, keepdims=True)
    lb = jnp.sum(sm[0:layer + 1], axis=0, keepdims=True) - sm[0:1]

    ti = lax.broadcasted_iota(jnp.int32, (c, c), 0)
    si = lax.broadcasted_iota(jnp.int32, (c, c), 1)
    txs = ti ^ si
    lower = ti > si
    ones_c = jnp.ones((c, LANES), F32)

    for ib in range(nb):
        rows = pl.ds(ib * c, c)
        hq, hf, v = hq_ref[rows, :], hf_ref[rows, :], hi_ref[rows, :]
        sig = jax.nn.sigmoid(hf)
        f_gate = lb + (1.0 - lb) * sig
        logf = jnp.log(jnp.maximum(f_gate, F_MIN))
        k = (1.0 - lb) * (1.0 - sig)
        q = jax.nn.silu(hq) * HG_DK ** -0.5
        lv = jnp.dot(ms_ref[...], logf, precision=HIGHEST, preferred_element_type=F32)
        bl_col = lax.dot_general(logf, ones_c, (((0,), (0,)), ((), ())), precision=HIGHEST,
                                 preferred_element_type=F32)
        qk = q * k

        def head(h):
            hs = slice(h * HG_DK, (h + 1) * HG_DK)
            qh, kh, vh, b = q[:, hs], k[:, hs], v[:, hs], lv[0:c, hs]
            att = jnp.where(ti == si, jnp.sum(qk[:, hs], axis=-1, keepdims=True), 0.0)
            for p in range(nlv):
                ex = jnp.exp(lv[(p + 1) * c:(p + 2) * c, hs])
                att = att + jnp.where(((txs >> p) == 1) & lower, _dot_nt(qh * ex, kh * ex), 0.0)
            yield
            s = s_scr[ib, h]
            o = _dot(att, vh) + _dot(qh * jnp.exp(b), s)
            bl = b[c - 1:c]
            s_scr[ib, h] = jnp.exp(bl_col[hs, :]) * s + _dot_tn(kh * jnp.exp(bl - b), vh)
            yield
            y_ref[rows, hs] = _rms(o) * gn_ref[...]

        _round_robin([head(h) for h in range(HG_HEADS)])

    @pl.when(ci == pl.num_programs(1) - 1)
    def _():
        so_ref[...] = s_scr[...]


def _hgrn(z, grp, layer, lower_bounds, gnorm, state, into):
    c = min(CHUNK, grp.seq)
    nb = 1 if grp.seq > c else 2
    nchunk = grp.seq // c
    blk = nb * c
    base = grp.row0 // blk

    def zspec(col0):
        return pl.BlockSpec((blk, BR_W), lambda i, ci: (base + i * nchunk + ci, col0 // BR_W))

    in_specs = [zspec(Z_HG), zspec(Z_HG + BR_W), zspec(Z_HG + 2 * BR_W),
                pl.BlockSpec((DEPTH, BR_W), lambda i, ci: (0, 0)),
                _layer_block(layer, (1, HG_DV), lambda i, ci: (0, 0)),
                pl.BlockSpec(((c.bit_length()) * c, c), lambda i, ci: (0, 0))]
    args = [z, z, z, lower_bounds, gnorm, jnp.asarray(_gla_level_matrix(c))]
    st_spec = _layer_block(layer, (nb, HG_HEADS, HG_DK, HG_DV), lambda i, ci: (i, 0, 0, 0))
    if grp.has_state:
        in_specs.append(st_spec)
        args.append(state)
    return _pcall(
        functools.partial(_hgrn_kernel, c=c, nb=nb, layer=layer, has_state=grp.has_state),
        name="hgrn_" + grp.name, grid=(grp.batch // nb, nchunk), in_specs=in_specs, args=args,
        out_specs=[pl.BlockSpec((blk, BR_W), lambda i, ci: (base + i * nchunk + ci, 0)), st_spec],
        out_shape=[jax.ShapeDtypeStruct((N_ROWS, BR_W), F32),
                   jax.ShapeDtypeStruct((DEPTH, grp.batch, HG_HEADS, HG_DK, HG_DV), F32)],
        into=into, scratch=[pltpu.VMEM((nb, HG_HEADS, HG_DK, HG_DV), F32)],
        semantics=("parallel", "arbitrary"), vmem_mb=48)


def _rope_kernel(inv_ref, cos_ref, sin_ref, *, pos0):
    t = cos_ref.shape[0]
    pos = (pos0 + lax.broadcasted_iota(jnp.int32, (t, RN_DK // 2), 0)).astype(F32)
    ang = pos * inv_ref[...]
    cos_ref[...] = jnp.cos(ang)
    sin_ref[...] = jnp.sin(ang)


def _rope_tables(grp):
    inv = ROPE_BASE ** (-jnp.arange(0, RN_DK, 2, dtype=F32) / RN_DK)
    shp = jax.ShapeDtypeStruct((grp.seq, RN_DK // 2), F32)
    return pl.pallas_call(functools.partial(_rope_kernel, pos0=grp.pos0), out_shape=[shp, shp],
                          name="rope")(inv[None, :])


def _ret_kernel(*refs, c, nb, has_state):
    if has_state:
        (q_ref, k_ref, v_ref, cos_ref, sin_ref, dm_ref, qd_ref, kd_ref, cd_ref, s0_ref,
         y_ref, so_ref, s_scr) = refs
    else:
        (q_ref, k_ref, v_ref, cos_ref, sin_ref, dm_ref, qd_ref, kd_ref, cd_ref,
         y_ref, so_ref, s_scr) = refs
    ci = pl.program_id(1)

    @pl.when(ci == 0)
    def _():
        if has_state:
            s_scr[...] = s0_ref[...]
        else:
            s_scr[...] = jnp.zeros_like(s_scr)

    cos, sin = cos_ref[...], sin_ref[...]
    half = RN_DK // 2

    def rot(x):
        x1, x2 = x[:, :half], x[:, half:]
        return jnp.concatenate([x1 * cos - x2 * sin, x1 * sin + x2 * cos], axis=1)

    def head(ib, h):
        rows, hs = pl.ds(ib * c, c), slice(h * RN_DK, (h + 1) * RN_DK)
        q = rot(q_ref[rows, hs])
        k = rot(k_ref[rows, hs]) * RN_DK ** -0.5
        v = v_ref[rows, hs]
        s = s_scr[ib, h]
        att = _dot_nt(q, k) * dm_ref[h]
        qs = _dot(q, s)
        s_scr[ib, h] = cd_ref[h] * s + _dot_tn(k * kd_ref[h], v)
        yield
        o = _dot(att, v) + qs * qd_ref[h]
        yield
        y_ref[rows, hs] = _rms(o)

    _round_robin([head(ib, h) for ib in range(nb) for h in range(RN_HEADS)])

    @pl.when(ci == pl.num_programs(1) - 1)
    def _():
        so_ref[...] = s_scr[...]


def _retention(z, grp, tables, state, layer, into):
    c = min(CHUNK, grp.seq)
    nb = 1 if grp.seq > c else 2
    nchunk = grp.seq // c
    blk = nb * c
    base = grp.row0 // blk
    log_gamma = jnp.log(1.0 - 2.0 ** (-5.0 - jnp.arange(RN_HEADS, dtype=F32)))
    j = jnp.arange(c, dtype=F32)
    rel = j[:, None] - j[None, :]
    dmat = jnp.where(rel >= 0, jnp.exp(log_gamma[:, None, None] * jnp.maximum(rel, 0.0)), 0.0)
    q_dec = jnp.exp(log_gamma[:, None] * (j + 1.0))[..., None]
    k_dec = jnp.exp(log_gamma[:, None] * (c - 1.0 - j))[..., None]
    c_dec = jnp.exp(log_gamma * c)[:, None, None]

    def zspec(col0):
        return pl.BlockSpec((blk, BR_W), lambda i, ci: (base + i * nchunk + ci, col0 // BR_W))

    def const(shape):
        return pl.BlockSpec(shape, lambda i, ci: (0,) * len(shape))

    tspec = pl.BlockSpec((c, RN_DK // 2), lambda i, ci: (ci, 0))
    in_specs = [zspec(Z_RN), zspec(Z_RN + BR_W), zspec(Z_RN + 2 * BR_W), tspec, tspec,
                const((RN_HEADS, c, c)), const((RN_HEADS, c, 1)), const((RN_HEADS, c, 1)), const((RN_HEADS, 1, 1))]
    args = [z, z, z, tables[0], tables[1], dmat, q_dec, k_dec, c_dec]
    st_spec = _layer_block(layer, (nb, RN_HEADS, RN_DK, RN_DV), lambda i, ci: (i, 0, 0, 0))
    if grp.has_state:
        in_specs.append(st_spec)
        args.append(state)
    return _pcall(
        functools.partial(_ret_kernel, c=c, nb=nb, has_state=grp.has_state),
        name="ret_" + grp.name, grid=(grp.batch // nb, nchunk), in_specs=in_specs, args=args,
        out_specs=[pl.BlockSpec((blk, BR_W), lambda i, ci: (base + i * nchunk + ci, 0)), st_spec],
        out_shape=[jax.ShapeDtypeStruct((N_ROWS, BR_W), F32),
                   jax.ShapeDtypeStruct((DEPTH, grp.batch, RN_HEADS, RN_DK, RN_DV), F32)],
        into=into, scratch=[pltpu.VMEM((nb, RN_HEADS, RN_DK, RN_DV), F32)],
        semantics=("parallel", "arbitrary"), vmem_mb=48)


RW_PREP_ROWS = 256


def _rw_prep_kernel(*refs, tt, nb, has_state, with_vmix):
    refs = list(refs)
    zr_ref, zl_ref = refs[:2]
    pos = 2
    if has_state:
        sr_ref, sl_ref = refs[pos:pos + 2]
        pos += 2
    (mur_ref, mul_ref, w0_ref, w2_ref, a0_ref, a2_ref, kk_ref, ka_ref) = refs[pos:pos + 8]
    pos += 8
    if with_vmix:
        v0_ref, v1_ref, v2_ref, vf_ref = refs[pos:pos + 4]
        pos += 4
    r_ref, d_ref, k_ref, v_ref, a_ref, b_ref = refs[pos:pos + 6]
    pr_scr, pl_scr = refs[pos + 6:pos + 8]
    ti = pl.program_id(1)
    rows = nb * tt

    zr, zl = zr_ref[...], zl_ref[...]
    first = (lax.broadcasted_iota(jnp.int32, (rows, 1), 0) % tt) == 0

    def prev_rows(z, st_ref, scr):
        w = z.shape[1]
        if nb > 1:
            st = st_ref[...]
            return jnp.broadcast_to(st[:, None, :], (nb, tt, w)).reshape(rows, w)

        @pl.when(ti == 0)
        def _():
            scr[0:1, :] = st_ref[...] if has_state else jnp.zeros((1, w), F32)

        return scr[0:1, :]

    def shifted(z, st_ref, scr, mu):
        prev = jnp.where(first, prev_rows(z, st_ref, scr), pltpu.roll(z, 1, 0))
        scr[0:1, :] = z[rows - 1:rows, :]
        return z + (prev - z) * mu

    ur = shifted(zr, sr_ref if has_state else None, pr_scr, mur_ref[...])
    ul = shifted(zl, sl_ref if has_state else None, pl_scr, mul_ref[...])

    r, k, v = ur[:, :BR_W], ur[:, BR_W:2 * BR_W], ur[:, 2 * BR_W:]
    wlo, alo = ul[:, :LORA_PAD], ul[:, LORA_PAD:]
    w = -jax.nn.softplus(-(w0_ref[...] + _dot(jnp.tanh(wlo), w2_ref[...]))) - 0.5
    a = jax.nn.sigmoid(a0_ref[...] + _dot(alo, a2_ref[...]))
    kk = k * kk_ref[...]
    ones = _head_ones(MXU_DIM, RW_N)
    kk2 = kk * kk
    nrm2 = jnp.concatenate(
        [_split_dot(kk2[:, g * MXU_DIM:(g + 1) * MXU_DIM], ones) for g in range(BR_W // MXU_DIM)], axis=1)
    kk = kk / jnp.maximum(jnp.sqrt(nrm2), 1e-12)
    k = k * (1.0 + (a - 1.0) * ka_ref[...])
    if with_vmix:
        mix = jax.nn.sigmoid(v0_ref[...] + _dot(_dot(v, v1_ref[...]), v2_ref[...]))
        v = v + (vf_ref[...] - v) * mix
    r_ref[...] = r
    d_ref[...] = -jnp.exp(w)
    k_ref[...] = k
    v_ref[...] = v
    a_ref[...] = -kk
    b_ref[...] = kk * a


def _rw_prep(z, grp, prm, state_rkv, state_lora, v_first):
    tt = min(RW_PREP_ROWS, grp.seq)
    nb = RW_PREP_ROWS // tt
    ntile = grp.seq // tt
    rows = nb * tt
    base = grp.row0 // rows
    with_vmix = v_first is not None
    w_rkv, w_lora = 3 * BR_W, 2 * LORA_PAD

    def rowmap(i, t):
        return base + i * ntile + t

    def const(shape):
        return pl.BlockSpec(shape, lambda i, t: (0,) * len(shape))

    in_specs = [pl.BlockSpec((rows, w_rkv), lambda i, t: (rowmap(i, t), Z_RKV // w_rkv)),
                pl.BlockSpec((rows, w_lora), lambda i, t: (rowmap(i, t), Z_LORA // w_lora))]
    args = [z, z]
    if grp.has_state:
        in_specs += [pl.BlockSpec((nb, w_rkv), lambda i, t: (i, 0)),
                     pl.BlockSpec((nb, w_lora), lambda i, t: (i, 0))]
        args += [state_rkv, state_lora]
    in_specs += [const((1, w_rkv)), const((1, w_lora)), const((1, BR_W)), const((LORA_PAD, BR_W)),
                 const((1, BR_W)), const((LORA_PAD, BR_W)), const((1, BR_W)), const((1, BR_W))]
    args += [prm['mu_rkv'], prm['mu_lora'], prm['w0'], prm['w2'], prm['a0'], prm['a2'], prm['k_k'], prm['k_a']]
    out_spec = pl.BlockSpec((rows, BR_W), lambda i, t: (i * ntile + t, 0))
    if with_vmix:
        in_specs += [const((1, BR_W)), const((BR_W, LANES)), const((LANES, BR_W)), out_spec]
        args += [prm['v0'], prm['v1'], prm['v2'], v_first]
    out = jax.ShapeDtypeStruct((grp.rows, BR_W), F32)
    return pl.pallas_call(
        functools.partial(_rw_prep_kernel, tt=tt, nb=nb, has_state=grp.has_state, with_vmix=with_vmix),
        grid=(grp.batch // nb, ntile),
        in_specs=in_specs,
        out_specs=[out_spec] * 6,
        out_shape=[out] * 6,
        scratch_shapes=[pltpu.VMEM((SUBLANES, w_rkv), F32), pltpu.VMEM((SUBLANES, w_lora), F32)],
        compiler_params=pltpu.CompilerParams(
            dimension_semantics=("parallel", "arbitrary"), vmem_limit_bytes=48 << 20),
        name="rw_prep_" + ("sample" if grp.has_state else "prompt"),
    )(*args)


RW_TILE = MXU_DIM
RW_TILE_HEADS = RW_TILE // RW_N
RW_SCAN_NB = 4
RW_SCAN_NG = 2


def _rw_scan_kernel(r_ref, d_ref, k_ref, v_ref, a_ref, b_ref, lw_ref, lb_ref, rk_ref, s0_ref,
                    y_ref, so_ref, s_scr, y_scr, *, tt, nb, ng):
    chains = [(ib, g) for ib in range(nb) for g in range(ng)]

    for ib, g in chains:
        s_scr[ib, g] = jnp.concatenate(
            [s0_ref[ib, g * RW_TILE_HEADS + h] for h in range(RW_TILE_HEADS)], axis=1)

    ones = _head_ones(RW_TILE, RW_N)
    eye = (lax.broadcasted_iota(jnp.int32, (RW_N, RW_TILE), 0)
           == lax.broadcasted_iota(jnp.int32, (RW_N, RW_TILE), 1) % RW_N).astype(F32)

    def row(ref, ib, g, t):
        return ref[pl.ds(ib * tt + t, 1), g * RW_TILE:(g + 1) * RW_TILE]

    def step(t, carry):
        lhs = []
        for ib, g in chains:
            p = s_scr[ib, g] * row(a_ref, ib, g, t)
            hi = _bf(p)
            lhs += [hi, _bf(p - hi.astype(F32)), _bf(eye * row(v_ref, ib, g, t))]
        res = jnp.dot(jnp.concatenate(lhs, axis=0), ones, preferred_element_type=F32)
        lhs = []
        for n, (ib, g) in enumerate(chains):
            o = 3 * n * RW_N
            sa = res[o:o + RW_N] + res[o + RW_N:o + 2 * RW_N]
            vcol = res[o + 2 * RW_N:o + 3 * RW_N]
            s = (s_scr[ib, g] * jnp.exp(row(d_ref, ib, g, t)) + sa * row(b_ref, ib, g, t)
                 + vcol * row(k_ref, ib, g, t))
            s_scr[ib, g] = s
            lhs.append(_bf(s * row(r_ref, ib, g, t)))
        res = jnp.dot(jnp.concatenate(lhs, axis=0), ones, preferred_element_type=F32)
        for n, (ib, g) in enumerate(chains):
            yb = res[n * RW_N:(n + 1) * RW_N]
            y_scr[pl.ds(ib * tt + t, 1), g * RW_TILE:(g + 1) * RW_TILE] = jnp.sum(yb * eye, axis=0, keepdims=True)
        return carry

    lax.fori_loop(0, tt, step, 0)

    for g in range(ng):
        sl = slice(g * RW_TILE, (g + 1) * RW_TILE)
        y = y_scr[:, sl]
        mu = _split_dot(y, ones) * (1.0 / RW_N)
        dlt = y - mu
        var = _split_dot(dlt * dlt, ones) * (1.0 / RW_N)
        yn = dlt * lax.rsqrt(var + RW_GN_EPS) * lw_ref[:, sl] + lb_ref[:, sl]
        bonus = _split_dot(r_ref[:, sl] * k_ref[:, sl] * rk_ref[:, sl], ones)
        y_ref[:, sl] = yn + bonus * v_ref[:, sl]

    for ib, g in chains:
        s = s_scr[ib, g]
        for h in range(RW_TILE_HEADS):
            so_ref[ib, g * RW_TILE_HEADS + h] = s[:, h * RW_N:(h + 1) * RW_N]


def _rw_scan(seqs, grp, norm, state, layer, into):
    assert grp.has_state
    tt, nb, ng = grp.seq, RW_SCAN_NB, RW_SCAN_NG
    width, blk = ng * RW_TILE, nb * grp.seq
    nh = ng * RW_TILE_HEADS
    base = grp.row0 // blk
    seq_spec = pl.BlockSpec((blk, width), lambda i, j: (i, j))
    par_spec = pl.BlockSpec((1, width), lambda i, j: (0, j))
    st_spec = _layer_block(layer, (nb, nh, RW_N, RW_N), lambda i, j: (i, j, 0, 0))
    return _pcall(
        functools.partial(_rw_scan_kernel, tt=tt, nb=nb, ng=ng),
        name="rw_scan_" + grp.name, grid=(grp.batch // nb, BR_W // width),
        in_specs=[seq_spec] * 6 + [par_spec] * 3 + [st_spec], args=list(seqs) + list(norm) + [state],
        out_specs=[pl.BlockSpec((blk, width), lambda i, j: (base + i, j)), st_spec],
        out_shape=[jax.ShapeDtypeStruct((N_ROWS, BR_W), F32),
                   jax.ShapeDtypeStruct((DEPTH, grp.batch, RW_HEADS, RW_N, RW_N), F32)],
        into=into, scratch=[pltpu.VMEM((nb, ng, RW_N, RW_TILE), F32), pltpu.VMEM((blk, width), F32)],
        semantics=("parallel", "parallel"), vmem_mb=48)


RW_CHUNK = RW_N
RW_CHUNKS_PER_STEP = 2


def _rw_chunk_kernel(r_ref, d_ref, k_ref, v_ref, a_ref, b_ref, lw_ref, lb_ref, rk_ref, y_ref, ho_ref, h_scr):
    ci = pl.program_id(1)
    ln, tw, nt = RW_CHUNK, RW_TILE, BR_W // RW_TILE

    @pl.when(ci == 0)
    def _():
        h_scr[...] = jnp.zeros_like(h_scr)

    row = lax.broadcasted_iota(jnp.int32, (tw, tw), 0)
    col = lax.broadcasted_iota(jnp.int32, (tw, tw), 1)
    same_head = (row // RW_N) == (col // RW_N)
    ones = same_head.astype(BF16)
    t_i = lax.broadcasted_iota(jnp.int32, (ln, tw), 0)
    s_i = lax.broadcasted_iota(jnp.int32, (ln, tw), 1) % ln
    strict, incl = s_i < t_i, s_i <= t_i
    tril = (lax.broadcasted_iota(jnp.int32, (ln, ln), 0) >= lax.broadcasted_iota(jnp.int32, (ln, ln), 1)).astype(F32)

    def blocks(x):
        return jnp.where(same_head, jnp.concatenate([x] * RW_TILE_HEADS, axis=0), 0.0).astype(BF16)

    def mm(x, w):
        return jnp.dot(_bf(x), w, preferred_element_type=F32)

    def prepare(j, g, out):
        rows, sl = pl.ds(j * ln, ln), slice(g * tw, (g + 1) * tw)
        r, ld, k, v, a, b = (ref[rows, sl] for ref in (r_ref, d_ref, k_ref, v_ref, a_ref, b_ref))
        c = jnp.dot(tril, ld, precision=HIGHEST, preferred_element_type=F32)
        tot_col = lax.dot_general(ld, jnp.ones((ln, LANES), F32), (((0,), (0,)), ((), ())),
                                  precision=HIGHEST, preferred_element_type=F32)
        yield
        c_last = c[ln - 1:ln]
        e_inv, e_end = jnp.exp(-c), jnp.exp(c_last - c)
        at, rt = a * jnp.exp(c - ld), r * jnp.exp(c)
        bt, kt, bp, kp = b * e_inv, k * e_inv, b * e_end, k * e_end
        gram = lax.dot_general(_bf(jnp.concatenate([at, rt], axis=0)),
                               jnp.concatenate([blocks(bt), blocks(kt)], axis=0),
                               (((1,), (1,)), ((), ())), preferred_element_type=F32)
        yield
        n_ab = jnp.where(strict, gram[:ln, :tw], 0.0)
        a_ak = jnp.where(strict, gram[:ln, tw:], 0.0)
        a_rb = jnp.where(incl, gram[ln:, :tw], 0.0)
        a_rk = jnp.where(incl, gram[ln:, tw:], 0.0)
        av = mm(jnp.concatenate([a_ak, a_rk], axis=0), blocks(v))
        yield
        w = jnp.concatenate([at, av[:ln]], axis=1)
        p = n_ab
        for lvl in range(ln.bit_length() - 1):
            w = w + mm(p, jnp.concatenate([blocks(w[:, :tw]), blocks(w[:, tw:])], axis=1))
            if lvl < ln.bit_length() - 2:
                p = mm(p, blocks(p))
            yield
        at2, uv = w[:, :tw], w[:, tw:]
        m_off = jnp.where(same_head, _dot_tn(bp, at2), 0.0)
        cc = jnp.where(same_head, _dot_tn(jnp.concatenate([bp, kp], axis=0), jnp.concatenate([uv, v], axis=0)), 0.0)
        qy = mm(a_rb, jnp.concatenate([blocks(at2), blocks(uv)], axis=1))
        decay_col = jnp.concatenate([jnp.exp(tot_col)] * (tw // LANES), axis=1)
        out[j, g] = (rt + qy[:, :tw], qy[:, tw:] + av[ln:], m_off, cc, decay_col)

    items = [(j, g) for j in range(RW_CHUNKS_PER_STEP) for g in range(nt)]
    prepared = {}
    _round_robin([prepare(j, g, prepared) for j, g in items])

    for j, g in items:
        q, yc, m_off, cc, decay_col = prepared[(j, g)]
        rows, sl = pl.ds(j * ln, ln), slice(g * tw, (g + 1) * tw)
        h = h_scr[g]
        hb = _bf(h)
        y = mm(q, hb) + yc
        h_scr[g] = decay_col * h + mm(m_off, hb) + cc

        mu = _split_dot(y, ones) * (1.0 / RW_N)
        dlt = y - mu
        var = _split_dot(dlt * dlt, ones) * (1.0 / RW_N)
        yn = dlt * lax.rsqrt(var + RW_GN_EPS) * lw_ref[:, sl] + lb_ref[:, sl]
        y_ref[rows, sl] = yn + _split_dot(r_ref[rows, sl] * k_ref[rows, sl] * rk_ref[:, sl], ones) * v_ref[rows, sl]

    @pl.when(ci == pl.num_programs(1) - 1)
    def _():
        for g in range(nt):
            h = h_scr[g]
            for hd in range(RW_TILE_HEADS):
                ho_ref[0, g * RW_TILE_HEADS + hd] = h[hd * RW_N:(hd + 1) * RW_N, hd * RW_N:(hd + 1) * RW_N]


def _rw_scan_chunked(seqs, grp, norm, layer, into):
    blk = RW_CHUNK * RW_CHUNKS_PER_STEP
    assert not grp.has_state and grp.seq % blk == 0 and grp.row0 % blk == 0
    nstep = grp.seq // blk
    base = grp.row0 // blk
    seq_spec = pl.BlockSpec((blk, BR_W), lambda i, c: (i * nstep + c, 0))
    par_spec = pl.BlockSpec((1, BR_W), lambda i, c: (0, 0))
    return _pcall(
        _rw_chunk_kernel, name="rw_chunk_" + grp.name, grid=(grp.batch, nstep),
        in_specs=[seq_spec] * 6 + [par_spec] * 3, args=list(seqs) + list(norm),
        out_specs=[pl.BlockSpec((blk, BR_W), lambda i, c: (base + i * nstep + c, 0)),
                   _layer_block(layer, (1, RW_HEADS, RW_N, RW_N), lambda i, c: (i, 0, 0, 0))],
        out_shape=[jax.ShapeDtypeStruct((N_ROWS, BR_W), F32),
                   jax.ShapeDtypeStruct((DEPTH, grp.batch, RW_HEADS, RW_N, RW_N), F32)],
        into=into, scratch=[pltpu.VMEM((BR_W // RW_TILE, RW_TILE, RW_TILE), F32)],
        semantics=("parallel", "arbitrary"), vmem_mb=48)


MERGE_TM = 256


def _merge_kernel(y0_ref, y1_ref, y2_ref, g_ref, zm_ref, wb_ref, m_ref):
    acc = None
    for n, y_ref in enumerate((y0_ref, y1_ref, y2_ref)):
        ys = y_ref[...] * jax.nn.silu(g_ref[:, n * BR_W:(n + 1) * BR_W])
        proj = jnp.dot(_bf(ys), wb_ref[n], preferred_element_type=F32)
        term = jax.nn.sigmoid(zm_ref[:, n * D_MODEL:(n + 1) * D_MODEL]) * proj
        acc = term if acc is None else acc + term
    m_ref[...] = _bf(acc)


def _merge(ys, z, wb, layer):
    n = z.shape[0]
    yspec = pl.BlockSpec((MERGE_TM, BR_W), lambda i: (i, 0))
    gw, mw = N_BRANCH * BR_W, N_BRANCH * D_MODEL
    return _pcall(
        _merge_kernel, name="merge", grid=(n // MERGE_TM,),
        in_specs=[yspec, yspec, yspec,
                  pl.BlockSpec((MERGE_TM, gw), lambda i: (i, Z_GATE // gw)),
                  pl.BlockSpec((MERGE_TM, mw), lambda i: (i, Z_MERGE // mw)),
                  pl.BlockSpec((None, N_BRANCH, BR_W, D_MODEL), lambda i: (layer, 0, 0, 0),
                               pipeline_mode=pl.Buffered(1))],
        args=[*ys, z, z, wb],
        out_specs=pl.BlockSpec((MERGE_TM, D_MODEL), lambda i: (i, 0)),
        out_shape=jax.ShapeDtypeStruct((n, D_MODEL), BF16),
        semantics=("parallel",), vmem_mb=48)


POST_TM = 256


def _post_kernel(x_ref, m_ref, p_ref, wo_ref, np_ref, pu_ref, pn_ref, pg_ref, o_ref):
    x = x_ref[...] + _rms(jnp.dot(m_ref[...], wo_ref[...], preferred_element_type=F32)) * np_ref[...]
    e = _rms(jnp.dot(_bf(p_ref[...]), pu_ref[...], preferred_element_type=F32)) * pn_ref[...]
    o_ref[...] = x + e * jax.nn.sigmoid(jnp.dot(_bf(x), pg_ref[...], preferred_element_type=F32))


def _post(x, m, p, wo, norm_post, ple_up, ple_norm, ple_gate, layer):
    n = x.shape[0]

    def rows(w):
        return pl.BlockSpec((POST_TM, w), lambda i: (i, 0))

    def const(shape):
        return pl.BlockSpec((None,) + shape, lambda i: (layer, 0, 0), pipeline_mode=pl.Buffered(1))

    return _pcall(
        _post_kernel, name="post", grid=(n // POST_TM,),
        in_specs=[rows(D_MODEL), rows(D_MODEL), _layer_block(layer, (POST_TM, PLE_DIM), lambda i: (i, 0)),
                  const((D_MODEL, D_MODEL)), const((1, D_MODEL)), const((PLE_DIM, D_MODEL)),
                  const((1, D_MODEL)), const((D_MODEL, D_MODEL))],
        args=[x, m, p, wo, norm_post, ple_up, ple_norm, ple_gate],
        out_specs=rows(D_MODEL), out_shape=jax.ShapeDtypeStruct((n, D_MODEL), F32),
        semantics=("parallel",), vmem_mb=48)


def _pad_rows(a, rows):
    return jnp.concatenate([a, jnp.zeros((rows - a.shape[0],) + a.shape[1:], a.dtype)], axis=0)


def _pad_cols(a, cols):
    return jnp.concatenate([a, jnp.zeros(a.shape[:-1] + (cols - a.shape[-1],), a.dtype)], axis=-1)


def _layer_rows(a):
    return a.reshape(a.shape[0], 1, a.shape[1])


def kernel(x_prompt, x_sample, state_hgrn, state_rwkv, state_shift, state_ret, p_prompt, p_sample,
           norm_pre, w_in, hg_lower_bounds, hg_norm, rw_mu, rw_w0, rw_w2, rw_a0, rw_a2, rw_k_k, rw_k_a,
           rw_v0, rw_v1, rw_v2, rw_r_k, rw_ln_w, rw_ln_b, w_branch, w_out, norm_post, ple_up, ple_norm, ple_gate):
    groups = (PROMPT, SAMPLE)
    x = jnp.concatenate([x_prompt.reshape(N_PROMPT, D_MODEL), x_sample.reshape(N_SAMPLE, D_MODEL)], axis=0)
    p = jnp.concatenate([p_prompt.reshape(DEPTH, N_PROMPT, PLE_DIM), p_sample.reshape(DEPTH, N_SAMPLE, PLE_DIM)], axis=1)
    w_packed = _pack_w_in(w_in)
    wb, wo, pu, pg = _bf(w_branch), _bf(w_out), _bf(ple_up), _bf(ple_gate)
    g_pre, g_hg, g_post, g_ple = (_layer_rows(a) for a in (norm_pre, hg_norm, norm_post, ple_norm))
    tables = [_rope_tables(g) for g in groups]
    v_first = [None, None]
    shifts = [[], []]
    st_out = [[None, None, None] for _ in groups]

    for l in range(DEPTH):
        z = _inproj(x, g_pre, w_packed, l)
        mu_rkv, mu_lora = _pack_shift_cols(rw_mu[l][None])
        prm = dict(mu_rkv=mu_rkv, mu_lora=mu_lora, w0=rw_w0[l][None],
                   w2=_bf(_pad_rows(rw_w2[l], LORA_PAD)), a0=rw_a0[l][None],
                   a2=_bf(_pad_rows(rw_a2[l], LORA_PAD)), k_k=rw_k_k[l][None], k_a=rw_k_a[l][None])
        if l > 0:
            prm.update(v0=rw_v0[l - 1][None], v1=_bf(_pad_cols(rw_v1[l - 1], LANES)),
                       v2=_bf(_pad_rows(rw_v2[l - 1], LANES)))
        rw_norm = (rw_ln_w[l][None], rw_ln_b[l][None], rw_r_k[l].reshape(1, BR_W))
        ys = [None, None, None]
        for gi, grp in enumerate(groups):
            st_rkv, st_lora = _pack_shift_cols(state_shift[l]) if grp.has_state else (None, None)
            ys[0], st_out[gi][0] = _hgrn(z, grp, l, hg_lower_bounds, g_hg, state_hgrn, (ys[0], st_out[gi][0]))
            seqs = _rw_prep(z, grp, prm, st_rkv, st_lora, v_first[gi])
            if l == 0:
                v_first[gi] = seqs[3]
            if grp.has_state:
                ys[1], st_out[gi][1] = _rw_scan(seqs, grp, rw_norm, state_rwkv, l, (ys[1], st_out[gi][1]))
            else:
                ys[1], st_out[gi][1] = _rw_scan_chunked(seqs, grp, rw_norm, l, (ys[1], st_out[gi][1]))
            ys[2], st_out[gi][2] = _retention(z, grp, tables[gi], state_ret, l, (ys[2], st_out[gi][2]))
            shifts[gi].append(_unpack_shift_rows(z[grp.row0 + grp.seq - 1:grp.row0 + grp.rows:grp.seq]))
        m = _merge(ys, z, wb, l)
        x = _post(x, m, p, wo, g_post, pu, g_ple, pg, l)

    y_prompt = x[:N_PROMPT].reshape(BATCH, SEQ, D_MODEL)
    y_sample = x[N_PROMPT:].reshape(DEC_BATCH, DEC_SEQ, D_MODEL)
    (hg_p, rw_p, rn_p), (hg_s, rw_s, rn_s) = st_out
    rw_p = jnp.swapaxes(rw_p, -1, -2)
    sh_p, sh_s = (jnp.stack(s, axis=0) for s in shifts)
    return (y_prompt, y_sample, hg_p, hg_s, rw_p, rw_s, sh_p, sh_s, rn_p, rn_s)
```

```python
import functools

import numpy as np
import jax
import jax.numpy as jnp
from jax import lax
from jax.experimental import pallas as pl
from jax.experimental.pallas import tpu as pltpu

F32 = jnp.float32
BF16 = jnp.bfloat16
HIGHEST = lax.Precision.HIGHEST

D_MODEL = 2048
BATCH, SEQ = 4, 2048
DEC_BATCH, DEC_SEQ = 128, 8
DEPTH = 2
PAST_LEN = 16384
N_BRANCH = 3
BR_W = 1024
HG_HEADS, HG_DK, HG_DV = 8, 128, 128
F_MIN = 1e-30
RW_HEADS, RW_N = 16, 64
RW_LORA = 96
RW_MV_LORA = 64
RW_GN_EPS = 64e-5
RN_HEADS, RN_DK, RN_DV = 4, 256, 256
ROPE_BASE = 10000.0
CHUNK = 64
PLE_DIM = 256
EPS = 1e-6

LANES = 128
SUBLANES = 8
MXU_DIM = 256

LORA_PAD = LANES
Z_HG = 0
Z_RKV = Z_HG + 3 * BR_W
Z_RN = Z_RKV + 3 * BR_W
Z_GATE = Z_RN + 3 * BR_W
Z_MERGE = Z_GATE + N_BRANCH * BR_W
Z_LORA = Z_MERGE + N_BRANCH * D_MODEL
Z_USED = Z_LORA + 2 * LORA_PAD
INPROJ_TN = 512
Z_W = -(-Z_USED // INPROJ_TN) * INPROJ_TN
INPROJ_TM = 1024

N_PROMPT = BATCH * SEQ
N_SAMPLE = DEC_BATCH * DEC_SEQ
N_ROWS = N_PROMPT + N_SAMPLE


def _bf(x):
    return x.astype(BF16)


def _dot(a, b):
    return jnp.dot(_bf(a), _bf(b), preferred_element_type=F32)


def _dot_nt(a, b):
    return lax.dot_general(_bf(a), _bf(b), (((1,), (1,)), ((), ())), preferred_element_type=F32)


def _dot_tn(a, b):
    return lax.dot_general(_bf(a), _bf(b), (((0,), (0,)), ((), ())), preferred_element_type=F32)


def _split_dot(x, g):
    hi = _bf(x)
    lo = _bf(x - hi.astype(F32))
    return (jnp.dot(hi, g, preferred_element_type=F32) + jnp.dot(lo, g, preferred_element_type=F32))


def _head_ones(width, head):
    r = lax.broadcasted_iota(jnp.int32, (width, width), 0) // head
    c = lax.broadcasted_iota(jnp.int32, (width, width), 1) // head
    return (r == c).astype(BF16)


def _rms(x):
    return x * lax.rsqrt(jnp.mean(x * x, axis=-1, keepdims=True) + EPS)


def _round_robin(gens):
    live = list(gens)
    while live:
        nxt = []
        for gen in live:
            try:
                next(gen)
                nxt.append(gen)
            except StopIteration:
                pass
        live = nxt


class _Group:
    def __init__(self, name, row0, batch, seq, pos0, has_state):
        self.name, self.row0, self.batch, self.seq, self.pos0, self.has_state = name, row0, batch, seq, pos0, has_state
        self.rows = batch * seq


PROMPT = _Group("prompt", 0, BATCH, SEQ, 0, False)
SAMPLE = _Group("sample", N_PROMPT, DEC_BATCH, DEC_SEQ, PAST_LEN, True)


def _pcall(body, *, name, grid, in_specs, args, out_specs, out_shape, into=None, scratch=(), semantics,
           vmem_mb=None):
    in_specs, args = list(in_specs), list(args)
    n_in, aliases = len(args), {}
    for k, arr in enumerate(into or ()):
        if arr is not None:
            aliases[len(args)] = k
            in_specs.append(pl.BlockSpec(memory_space=pl.ANY))
            args.append(arr)
    n_alias = len(aliases)

    def kernel_fn(*refs):
        return body(*refs[:n_in], *refs[n_in + n_alias:])

    params = dict(dimension_semantics=semantics)
    if vmem_mb is not None:
        params['vmem_limit_bytes'] = vmem_mb << 20
    return pl.pallas_call(
        kernel_fn, grid=grid, in_specs=in_specs, out_specs=out_specs, out_shape=out_shape,
        scratch_shapes=list(scratch), input_output_aliases=aliases,
        compiler_params=pltpu.CompilerParams(**params), name=name)(*args)


def _layer_block(layer, shape, index_map):
    return pl.BlockSpec((None,) + tuple(shape), lambda *g: (layer,) + tuple(index_map(*g)))


D_IN = 3 * BR_W + (3 * BR_W + 2 * RW_LORA) + 3 * BR_W + N_BRANCH * BR_W + N_BRANCH * D_MODEL
PACK_ROWS = 128
PACK_COLS = 2048


def _pack_kernel(w_ref, o_ref):
    o_rw = 3 * BR_W

    def copy(dst, src, n):
        for c in range(0, n, PACK_COLS):
            m = min(PACK_COLS, n - c)
            o_ref[:, dst + c:dst + c + m] = _bf(w_ref[src + c:src + c + m, :].T)

    def lora(src):
        t = w_ref[src:src + LORA_PAD, :].T
        keep = lax.broadcasted_iota(jnp.int32, t.shape, 1) < RW_LORA
        return _bf(jnp.where(keep, t, 0.0))

    copy(Z_HG, 0, o_rw + BR_W)
    copy(Z_RKV + BR_W, o_rw + BR_W + RW_LORA, 2 * BR_W)
    rest = o_rw + 3 * BR_W + 2 * RW_LORA
    copy(Z_RN, rest, D_IN - rest)
    o_ref[:, Z_LORA:Z_LORA + LORA_PAD] = lora(o_rw + BR_W)
    o_ref[:, Z_LORA + LORA_PAD:Z_USED] = lora(rest - RW_LORA)
    o_ref[:, Z_USED:] = jnp.zeros((PACK_ROWS, Z_W - Z_USED), BF16)


def _pack_w_in(w_in):
    return _pcall(
        _pack_kernel, name="pack_w_in", grid=(DEPTH, D_MODEL // PACK_ROWS),
        in_specs=[pl.BlockSpec((None, D_IN, PACK_ROWS), lambda l, i: (l, 0, i))], args=[jnp.swapaxes(w_in, 1, 2)],
        out_specs=pl.BlockSpec((None, PACK_ROWS, Z_W), lambda l, i: (l, i, 0)),
        out_shape=jax.ShapeDtypeStruct((DEPTH, D_MODEL, Z_W), BF16),
        semantics=("parallel", "parallel"), vmem_mb=48)


def _pack_shift_cols(a):
    r = a[..., :BR_W]
    wlo = a[..., BR_W:BR_W + RW_LORA]
    k = a[..., BR_W + RW_LORA:2 * BR_W + RW_LORA]
    v = a[..., 2 * BR_W + RW_LORA:3 * BR_W + RW_LORA]
    alo = a[..., 3 * BR_W + RW_LORA:]
    zl = jnp.zeros(a.shape[:-1] + (LORA_PAD - RW_LORA,), a.dtype)
    return jnp.concatenate([r, k, v], axis=-1), jnp.concatenate([wlo, zl, alo, zl], axis=-1)


def _unpack_shift_cols(rkv, lora):
    return jnp.concatenate([rkv[:, :BR_W], lora[:, :RW_LORA], rkv[:, BR_W:],
                            lora[:, LORA_PAD:LORA_PAD + RW_LORA]], axis=1)


def _inproj_kernel(x_ref, g_ref, w_ref, o_ref, h_ref):
    @pl.when(pl.program_id(1) == 0)
    def _():
        h_ref[...] = _bf(_rms(x_ref[...]) * g_ref[...])

    o_ref[...] = jnp.dot(h_ref[...], w_ref[...], preferred_element_type=F32)


def _inproj(x, g, w, layer):
    n = x.shape[0]
    return _pcall(
        _inproj_kernel, name="inproj", grid=(n // INPROJ_TM, Z_W // INPROJ_TN),
        in_specs=[pl.BlockSpec((INPROJ_TM, D_MODEL), lambda i, j: (i, 0)),
                  _layer_block(layer, (1, D_MODEL), lambda i, j: (0, 0)),
                  _layer_block(layer, (D_MODEL, INPROJ_TN), lambda i, j: (0, j))],
        args=[x, g, w],
        out_specs=pl.BlockSpec((INPROJ_TM, INPROJ_TN), lambda i, j: (i, j)),
        out_shape=jax.ShapeDtypeStruct((n, Z_W), F32),
        scratch=[pltpu.VMEM((INPROJ_TM, D_MODEL), BF16)],
        semantics=("parallel", "arbitrary"), vmem_mb=48)


def _gla_level_matrix(c):
    nlv = c.bit_length() - 1
    m = np.zeros(((nlv + 1) * c, c), np.float32)
    for r in range(c):
        m[r, :r + 1] = 1.0
        for p in range(nlv):
            bd = ((r >> (p + 1)) << (p + 1)) + (1 << p) - 1
            if (r >> p) & 1:
                m[(p + 1) * c + r, bd + 1:r + 1] = 1.0
            else:
                m[(p + 1) * c + r, r + 1:bd + 1] = 1.0
    return m


def _hgrn_kernel(*refs, c, nb, layer, has_state):
    if has_state:
        hq_ref, hf_ref, hi_ref, lbp_ref, gn_ref, ms_ref, s0_ref, y_ref, so_ref, s_scr = refs
    else:
        hq_ref, hf_ref, hi_ref, lbp_ref, gn_ref, ms_ref, y_ref, so_ref, s_scr = refs
    nlv = c.bit_length() - 1
    ci = pl.program_id(1)

    @pl.when(ci == 0)
    def _():
        if has_state:
            s_scr[...] = s0_ref[...]
        else:
            s_scr[...] = jnp.zeros_like(s_scr)

    lbp = lbp_ref[...]
    e = jnp.exp(lbp - jnp.max(lbp, axis=0, keepdims=True))
    sm = e / jnp.sum(e, axis=0, keepdims=True)
    lb = jnp.sum(sm[0:layer + 1], axis=0, keepdims=True) - sm[0:1]

    ti = lax.broadcasted_iota(jnp.int32, (c, c), 0)
    si = lax.broadcasted_iota(jnp.int32, (c, c), 1)
    txs = ti ^ si
    lower = ti > si
    ones_c = jnp.ones((c, LANES), F32)

    for ib in range(nb):
        rows = pl.ds(ib * c, c)
        hq, hf, v = hq_ref[rows, :], hf_ref[rows, :], hi_ref[rows, :]
        sig = jax.nn.sigmoid(hf)
        f_gate = lb + (1.0 - lb) * sig
        logf = jnp.log(jnp.maximum(f_gate, F_MIN))
        k = (1.0 - lb) * (1.0 - sig)
        q = jax.nn.silu(hq) * HG_DK ** -0.5
        lv = jnp.dot(ms_ref[...], logf, precision=HIGHEST, preferred_element_type=F32)
        bl_col = lax.dot_general(logf, ones_c, (((0,), (0,)), ((), ())), precision=HIGHEST,
                                 preferred_element_type=F32)
        qk = q * k

        def head(h):
            hs = slice(h * HG_DK, (h + 1) * HG_DK)
            qh, kh, vh, b = q[:, hs], k[:, hs], v[:, hs], lv[0:c, hs]
            att = jnp.where(ti == si, jnp.sum(qk[:, hs], axis=-1, keepdims=True), 0.0)
            for p in range(nlv):
                ex = jnp.exp(lv[(p + 1) * c:(p + 2) * c, hs])
                att = att + jnp.where(((txs >> p) == 1) & lower, _dot_nt(qh * ex, kh * ex), 0.0)
            yield
            s = s_scr[ib, h]
            o = _dot(att, vh) + _dot(qh * jnp.exp(b), s)
            bl = b[c - 1:c]
            s_scr[ib, h] = jnp.exp(bl_col[hs, :]) * s + _dot_tn(kh * jnp.exp(bl - b), vh)
            yield
            y_ref[rows, hs] = _rms(o) * gn_ref[...]

        _round_robin([head(h) for h in range(HG_HEADS)])

    @pl.when(ci == pl.num_programs(1) - 1)
    def _():
        so_ref[...] = s_scr[...]


def _hgrn(z, grp, layer, lower_bounds, gnorm, state, into):
    c = min(CHUNK, grp.seq)
    nb = 1 if grp.seq > c else 2
    nchunk = grp.seq // c
    blk = nb * c
    base = grp.row0 // blk

    def zspec(col0):
        return pl.BlockSpec((blk, BR_W), lambda i, ci: (base + i * nchunk + ci, col0 // BR_W))

    in_specs = [zspec(Z_HG), zspec(Z_HG + BR_W), zspec(Z_HG + 2 * BR_W),
                pl.BlockSpec((DEPTH, BR_W), lambda i, ci: (0, 0)),
                _layer_block(layer, (1, HG_DV), lambda i, ci: (0, 0)),
                pl.BlockSpec(((c.bit_length()) * c, c), lambda i, ci: (0, 0))]
    args = [z, z, z, lower_bounds, gnorm, jnp.asarray(_gla_level_matrix(c))]
    st_spec = _layer_block(layer, (nb, HG_HEADS, HG_DK, HG_DV), lambda i, ci: (i, 0, 0, 0))
    if grp.has_state:
        in_specs.append(st_spec)
        args.append(state)
    return _pcall(
        functools.partial(_hgrn_kernel, c=c, nb=nb, layer=layer, has_state=grp.has_state),
        name="hgrn_" + grp.name, grid=(grp.batch // nb, nchunk), in_specs=in_specs, args=args,
        out_specs=[pl.BlockSpec((blk, BR_W), lambda i, ci: (base + i * nchunk + ci, 0)), st_spec],
        out_shape=[jax.ShapeDtypeStruct((N_ROWS, BR_W), F32),
                   jax.ShapeDtypeStruct((DEPTH, grp.batch, HG_HEADS, HG_DK, HG_DV), F32)],
        into=into, scratch=[pltpu.VMEM((nb, HG_HEADS, HG_DK, HG_DV), F32)],
        semantics=("parallel", "arbitrary"), vmem_mb=48)


def _rope_kernel(inv_ref, cos_ref, sin_ref, *, pos0):
    t = cos_ref.shape[0]
    pos = (pos0 + lax.broadcasted_iota(jnp.int32, (t, RN_DK // 2), 0)).astype(F32)
    ang = pos * inv_ref[...]
    cos_ref[...] = jnp.cos(ang)
    sin_ref[...] = jnp.sin(ang)


def _rope_tables(grp):
    inv = ROPE_BASE ** (-jnp.arange(0, RN_DK, 2, dtype=F32) / RN_DK)
    shp = jax.ShapeDtypeStruct((grp.seq, RN_DK // 2), F32)
    return pl.pallas_call(functools.partial(_rope_kernel, pos0=grp.pos0), out_shape=[shp, shp],
                          name="rope")(inv[None, :])


def _ret_kernel(*refs, c, nb, has_state):
    if has_state:
        (q_ref, k_ref, v_ref, cos_ref, sin_ref, dm_ref, qd_ref, kd_ref, cd_ref, s0_ref,
         y_ref, so_ref, s_scr) = refs
    else:
        (q_ref, k_ref, v_ref, cos_ref, sin_ref, dm_ref, qd_ref, kd_ref, cd_ref,
         y_ref, so_ref, s_scr) = refs
    ci = pl.program_id(1)

    @pl.when(ci == 0)
    def _():
        if has_state:
            s_scr[...] = s0_ref[...]
        else:
            s_scr[...] = jnp.zeros_like(s_scr)

    cos, sin = cos_ref[...], sin_ref[...]
    half = RN_DK // 2

    def rot(x):
        x1, x2 = x[:, :half], x[:, half:]
        return jnp.concatenate([x1 * cos - x2 * sin, x1 * sin + x2 * cos], axis=1)

    def head(ib, h):
        rows, hs = pl.ds(ib * c, c), slice(h * RN_DK, (h + 1) * RN_DK)
        q = rot(q_ref[rows, hs])
        k = rot(k_ref[rows, hs]) * RN_DK ** -0.5
        v = v_ref[rows, hs]
        s = s_scr[ib, h]
        att = _dot_nt(q, k) * dm_ref[h]
        qs = _dot(q, s)
        s_scr[ib, h] = cd_ref[h] * s + _dot_tn(k * kd_ref[h], v)
        yield
        o = _dot(att, v) + qs * qd_ref[h]
        yield
        y_ref[rows, hs] = _rms(o)

    _round_robin([head(ib, h) for ib in range(nb) for h in range(RN_HEADS)])

    @pl.when(ci == pl.num_programs(1) - 1)
    def _():
        so_ref[...] = s_scr[...]


def _retention(z, grp, tables, state, layer, into):
    c = min(CHUNK, grp.seq)
    nb = 1 if grp.seq > c else 2
    nchunk = grp.seq // c
    blk = nb * c
    base = grp.row0 // blk
    log_gamma = jnp.log(1.0 - 2.0 ** (-5.0 - jnp.arange(RN_HEADS, dtype=F32)))
    j = jnp.arange(c, dtype=F32)
    rel = j[:, None] - j[None, :]
    dmat = jnp.where(rel >= 0, jnp.exp(log_gamma[:, None, None] * jnp.maximum(rel, 0.0)), 0.0)
    q_dec = jnp.exp(log_gamma[:, None] * (j + 1.0))[..., None]
    k_dec = jnp.exp(log_gamma[:, None] * (c - 1.0 - j))[..., None]
    c_dec = jnp.exp(log_gamma * c)[:, None, None]

    def zspec(col0):
        return pl.BlockSpec((blk, BR_W), lambda i, ci: (base + i * nchunk + ci, col0 // BR_W))

    def const(shape):
        return pl.BlockSpec(shape, lambda i, ci: (0,) * len(shape))

    tspec = pl.BlockSpec((c, RN_DK // 2), lambda i, ci: (ci, 0))
    in_specs = [zspec(Z_RN), zspec(Z_RN + BR_W), zspec(Z_RN + 2 * BR_W), tspec, tspec,
                const((RN_HEADS, c, c)), const((RN_HEADS, c, 1)), const((RN_HEADS, c, 1)), const((RN_HEADS, 1, 1))]
    args = [z, z, z, tables[0], tables[1], dmat, q_dec, k_dec, c_dec]
    st_spec = _layer_block(layer, (nb, RN_HEADS, RN_DK, RN_DV), lambda i, ci: (i, 0, 0, 0))
    if grp.has_state:
        in_specs.append(st_spec)
        args.append(state)
    return _pcall(
        functools.partial(_ret_kernel, c=c, nb=nb, has_state=grp.has_state),
        name="ret_" + grp.name, grid=(grp.batch // nb, nchunk), in_specs=in_specs, args=args,
        out_specs=[pl.BlockSpec((blk, BR_W), lambda i, ci: (base + i * nchunk + ci, 0)), st_spec],
        out_shape=[jax.ShapeDtypeStruct((N_ROWS, BR_W), F32),
                   jax.ShapeDtypeStruct((DEPTH, grp.batch, RN_HEADS, RN_DK, RN_DV), F32)],
        into=into, scratch=[pltpu.VMEM((nb, RN_HEADS, RN_DK, RN_DV), F32)],
        semantics=("parallel", "arbitrary"), vmem_mb=48)


RW_PREP_ROWS = 256


def _rw_prep_kernel(*refs, tt, nb, has_state, with_vmix):
    refs = list(refs)
    zr_ref, zl_ref = refs[:2]
    pos = 2
    if has_state:
        sr_ref, sl_ref = refs[pos:pos + 2]
        pos += 2
    (mur_ref, mul_ref, w0_ref, w2_ref, a0_ref, a2_ref, kk_ref, ka_ref) = refs[pos:pos + 8]
    pos += 8
    if with_vmix:
        v0_ref, v1_ref, v2_ref, vf_ref = refs[pos:pos + 4]
        pos += 4
    r_ref, d_ref, k_ref, v_ref, a_ref, b_ref, lastr_ref, lastl_ref = refs[pos:pos + 8]
    pr_scr, pl_scr = refs[pos + 8:pos + 10]
    ti = pl.program_id(1)
    rows = nb * tt

    zr, zl = zr_ref[...], zl_ref[...]
    for ib in range(nb):
        lastr_ref[ib] = zr_ref[pl.ds(ib * tt + tt - 1, 1), :]
        lastl_ref[ib] = zl_ref[pl.ds(ib * tt + tt - 1, 1), :]
    first = (lax.broadcasted_iota(jnp.int32, (rows, 1), 0) % tt) == 0

    def prev_rows(z, st_ref, scr):
        w = z.shape[1]
        if nb > 1:
            st = st_ref[...]
            return jnp.broadcast_to(st[:, None, :], (nb, tt, w)).reshape(rows, w)

        @pl.when(ti == 0)
        def _():
            scr[0:1, :] = st_ref[...] if has_state else jnp.zeros((1, w), F32)

        return scr[0:1, :]

    def shifted(z, st_ref, scr, mu):
        prev = jnp.where(first, prev_rows(z, st_ref, scr), pltpu.roll(z, 1, 0))
        scr[0:1, :] = z[rows - 1:rows, :]
        return z + (prev - z) * mu

    ur = shifted(zr, sr_ref if has_state else None, pr_scr, mur_ref[...])
    ul = shifted(zl, sl_ref if has_state else None, pl_scr, mul_ref[...])

    r, k, v = ur[:, :BR_W], ur[:, BR_W:2 * BR_W], ur[:, 2 * BR_W:]
    wlo, alo = ul[:, :LORA_PAD], ul[:, LORA_PAD:]
    w = -jax.nn.softplus(-(w0_ref[...] + _dot(jnp.tanh(wlo), w2_ref[...]))) - 0.5
    a = jax.nn.sigmoid(a0_ref[...] + _dot(alo, a2_ref[...]))
    kk = k * kk_ref[...]
    ones = _head_ones(MXU_DIM, RW_N)
    kk2 = kk * kk
    nrm2 = jnp.concatenate(
        [_split_dot(kk2[:, g * MXU_DIM:(g + 1) * MXU_DIM], ones) for g in range(BR_W // MXU_DIM)], axis=1)
    kk = kk / jnp.maximum(jnp.sqrt(nrm2), 1e-12)
    k = k * (1.0 + (a - 1.0) * ka_ref[...])
    if with_vmix:
        mix = jax.nn.sigmoid(v0_ref[...] + _dot(_dot(v, v1_ref[...]), v2_ref[...]))
        v = v + (vf_ref[...] - v) * mix
    r_ref[...] = r
    d_ref[...] = -jnp.exp(w)
    k_ref[...] = k
    v_ref[...] = v
    a_ref[...] = -kk
    b_ref[...] = kk * a


def _rw_prep(z, grp, prm, state_rkv, state_lora, v_first):
    tt = min(RW_PREP_ROWS, grp.seq)
    nb = RW_PREP_ROWS // tt
    ntile = grp.seq // tt
    rows = nb * tt
    base = grp.row0 // rows
    with_vmix = v_first is not None
    w_rkv, w_lora = 3 * BR_W, 2 * LORA_PAD

    def rowmap(i, t):
        return base + i * ntile + t

    def const(shape):
        return pl.BlockSpec(shape, lambda i, t: (0,) * len(shape))

    in_specs = [pl.BlockSpec((rows, w_rkv), lambda i, t: (rowmap(i, t), Z_RKV // w_rkv)),
                pl.BlockSpec((rows, w_lora), lambda i, t: (rowmap(i, t), Z_LORA // w_lora))]
    args = [z, z]
    if grp.has_state:
        in_specs += [pl.BlockSpec((nb, w_rkv), lambda i, t: (i, 0)),
                     pl.BlockSpec((nb, w_lora), lambda i, t: (i, 0))]
        args += [state_rkv, state_lora]
    in_specs += [const((1, w_rkv)), const((1, w_lora)), const((1, BR_W)), const((LORA_PAD, BR_W)),
                 const((1, BR_W)), const((LORA_PAD, BR_W)), const((1, BR_W)), const((1, BR_W))]
    args += [prm['mu_rkv'], prm['mu_lora'], prm['w0'], prm['w2'], prm['a0'], prm['a2'], prm['k_k'], prm['k_a']]
    out_spec = pl.BlockSpec((rows, BR_W), lambda i, t: (i * ntile + t, 0))
    if with_vmix:
        in_specs += [const((1, BR_W)), const((BR_W, LANES)), const((LANES, BR_W)), out_spec]
        args += [prm['v0'], prm['v1'], prm['v2'], v_first]
    out = jax.ShapeDtypeStruct((grp.rows, BR_W), F32)

    def last_spec(w):
        return pl.BlockSpec((nb, 1, w), lambda i, t: (i, 0, 0))

    def last_shape(w):
        return jax.ShapeDtypeStruct((grp.batch, 1, w), F32)

    return pl.pallas_call(
        functools.partial(_rw_prep_kernel, tt=tt, nb=nb, has_state=grp.has_state, with_vmix=with_vmix),
        grid=(grp.batch // nb, ntile),
        in_specs=in_specs,
        out_specs=[out_spec] * 6 + [last_spec(w_rkv), last_spec(w_lora)],
        out_shape=[out] * 6 + [last_shape(w_rkv), last_shape(w_lora)],
        scratch_shapes=[pltpu.VMEM((SUBLANES, w_rkv), F32), pltpu.VMEM((SUBLANES, w_lora), F32)],
        compiler_params=pltpu.CompilerParams(
            dimension_semantics=("parallel", "arbitrary"), vmem_limit_bytes=48 << 20),
        name="rw_prep_" + ("sample" if grp.has_state else "prompt"),
    )(*args)


RW_TILE = MXU_DIM
RW_TILE_HEADS = RW_TILE // RW_N
RW_LANE_HEADS = LANES // RW_N
RW_SCAN_UNROLL = 4


def _rw_scan_kernel(r_ref, d_ref, k_ref, v_ref, a_ref, b_ref, lw_ref, lb_ref, rk_ref, s0_ref,
                    y_ref, so_ref, s_scr, v_scr, y_scr):
    t = pl.program_id(1)

    @pl.when(t == 0)
    def _():
        s_scr[...] = s0_ref[...]

    r, ld, k, a, b = (ref[...].T for ref in (r_ref, d_ref, k_ref, a_ref, b_ref))
    v_scr[...] = v_ref[...].T
    w = jnp.exp(ld)

    for h in range(RW_LANE_HEADS):
        hs = slice(h * RW_N, (h + 1) * RW_N)
        rh, wh, kh, ah, bh = r[hs], w[hs], k[hs], a[hs], b[hs]

        def row_update(i, carry, h=h, rh=rh, wh=wh, kh=kh, ah=ah, bh=bh):
            s = s_scr[h, i]
            sa = jnp.sum(s * ah, axis=0, keepdims=True)
            s = s * wh + sa * bh + v_scr[pl.ds(h * RW_N + i, 1), :] * kh
            s_scr[h, i] = s
            y_scr[pl.ds(h * RW_N + i, 1), :] = jnp.sum(s * rh, axis=0, keepdims=True)
            return carry

        lax.fori_loop(0, RW_N, row_update, 0, unroll=RW_SCAN_UNROLL)

        y, vh = y_scr[hs, :], v_scr[hs, :]
        mu = jnp.mean(y, axis=0, keepdims=True)
        dlt = y - mu
        var = jnp.mean(dlt * dlt, axis=0, keepdims=True)
        yn = dlt * lax.rsqrt(var + RW_GN_EPS) * lw_ref[h] + lb_ref[h]
        bonus = jnp.sum(rh * kh * rk_ref[h], axis=0, keepdims=True)
        y_scr[hs, :] = yn + bonus * vh

    y_ref[...] = y_scr[...].T

    @pl.when(t == pl.num_programs(1) - 1)
    def _():
        so_ref[...] = s_scr[...]


def _rw_scan(seqs, grp, norm, state_t, layer, into):
    assert grp.has_state and grp.batch == LANES
    hp = RW_LANE_HEADS
    width = hp * RW_N
    tm = jnp.stack(seqs).reshape(len(seqs), grp.batch, grp.seq, BR_W).transpose(0, 2, 1, 3)
    seq_specs = [pl.BlockSpec((None, None, grp.batch, width), lambda j, t, n=n: (n, t, 0, j))
                 for n in range(len(seqs))]
    par_spec = pl.BlockSpec((hp, RW_N, 1), lambda j, t: (j, 0, 0))
    st_spec = _layer_block(layer, (hp, RW_N, RW_N, grp.batch), lambda j, t: (j, 0, 0, 0))
    y, st = _pcall(
        _rw_scan_kernel, name="rw_scan_" + grp.name, grid=(RW_HEADS // hp, grp.seq),
        in_specs=seq_specs + [par_spec] * 3 + [st_spec],
        args=[tm] * len(seqs) + [p.reshape(RW_HEADS, RW_N, 1) for p in norm] + [state_t],
        out_specs=[pl.BlockSpec((None, grp.batch, width), lambda j, t: (t, 0, j)), st_spec],
        out_shape=[jax.ShapeDtypeStruct((grp.seq, grp.batch, BR_W), F32),
                   jax.ShapeDtypeStruct((DEPTH, RW_HEADS, RW_N, RW_N, grp.batch), F32)],
        into=(None, into[1]),
        scratch=[pltpu.VMEM((hp, RW_N, RW_N, grp.batch), F32), pltpu.VMEM((width, grp.batch), F32),
                 pltpu.VMEM((width, grp.batch), F32)],
        semantics=("parallel", "arbitrary"), vmem_mb=48)
    y_rows = y.transpose(1, 0, 2).reshape(grp.rows, BR_W)
    return lax.dynamic_update_slice(into[0], y_rows, (grp.row0, 0)), st


RW_CHUNK = RW_N
RW_CHUNKS_PER_STEP = 2


def _rw_chunk_kernel(r_ref, d_ref, k_ref, v_ref, a_ref, b_ref, lw_ref, lb_ref, rk_ref, y_ref, ho_ref, h_scr):
    ci = pl.program_id(1)
    ln, tw, nt = RW_CHUNK, RW_TILE, BR_W // RW_TILE

    @pl.when(ci == 0)
    def _():
        h_scr[...] = jnp.zeros_like(h_scr)

    row = lax.broadcasted_iota(jnp.int32, (tw, tw), 0)
    col = lax.broadcasted_iota(jnp.int32, (tw, tw), 1)
    same_head = (row // RW_N) == (col // RW_N)
    ones = same_head.astype(BF16)
    t_i = lax.broadcasted_iota(jnp.int32, (ln, tw), 0)
    s_i = lax.broadcasted_iota(jnp.int32, (ln, tw), 1) % ln
    strict, incl = s_i < t_i, s_i <= t_i
    tril = (lax.broadcasted_iota(jnp.int32, (ln, ln), 0) >= lax.broadcasted_iota(jnp.int32, (ln, ln), 1)).astype(F32)

    def blocks(x):
        return jnp.where(same_head, jnp.concatenate([x] * RW_TILE_HEADS, axis=0), 0.0).astype(BF16)

    def mm(x, w):
        return jnp.dot(_bf(x), w, preferred_element_type=F32)

    def prepare(j, g, out):
        rows, sl = pl.ds(j * ln, ln), slice(g * tw, (g + 1) * tw)
        r, ld, k, v, a, b = (ref[rows, sl] for ref in (r_ref, d_ref, k_ref, v_ref, a_ref, b_ref))
        c = jnp.dot(tril, ld, precision=HIGHEST, preferred_element_type=F32)
        tot_col = lax.dot_general(ld, jnp.ones((ln, LANES), F32), (((0,), (0,)), ((), ())),
                                  precision=HIGHEST, preferred_element_type=F32)
        yield
        c_last = c[ln - 1:ln]
        e_inv, e_end = jnp.exp(-c), jnp.exp(c_last - c)
        at, rt = a * jnp.exp(c - ld), r * jnp.exp(c)
        bt, kt, bp, kp = b * e_inv, k * e_inv, b * e_end, k * e_end
        gram = lax.dot_general(_bf(jnp.concatenate([at, rt], axis=0)),
                               jnp.concatenate([blocks(bt), blocks(kt)], axis=0),
                               (((1,), (1,)), ((), ())), preferred_element_type=F32)
        yield
        n_ab = jnp.where(strict, gram[:ln, :tw], 0.0)
        a_ak = jnp.where(strict, gram[:ln, tw:], 0.0)
        a_rb = jnp.where(incl, gram[ln:, :tw], 0.0)
        a_rk = jnp.where(incl, gram[ln:, tw:], 0.0)
        av = mm(jnp.concatenate([a_ak, a_rk], axis=0), blocks(v))
        yield
        w = jnp.concatenate([at, av[:ln]], axis=1)
        p = n_ab
        for lvl in range(ln.bit_length() - 1):
            w = w + mm(p, jnp.concatenate([blocks(w[:, :tw]), blocks(w[:, tw:])], axis=1))
            if lvl < ln.bit_length() - 2:
                p = mm(p, blocks(p))
            yield
        at2, uv = w[:, :tw], w[:, tw:]
        m_off = jnp.where(same_head, _dot_tn(bp, at2), 0.0)
        cc = jnp.where(same_head, _dot_tn(jnp.concatenate([bp, kp], axis=0), jnp.concatenate([uv, v], axis=0)), 0.0)
        qy = mm(a_rb, jnp.concatenate([blocks(at2), blocks(uv)], axis=1))
        decay_col = jnp.concatenate([jnp.exp(tot_col)] * (tw // LANES), axis=1)
        out[j, g] = (rt + qy[:, :tw], qy[:, tw:] + av[ln:], m_off, cc, decay_col)

    items = [(j, g) for j in range(RW_CHUNKS_PER_STEP) for g in range(nt)]
    prepared = {}
    _round_robin([prepare(j, g, prepared) for j, g in items])

    for j, g in items:
        q, yc, m_off, cc, decay_col = prepared[(j, g)]
        rows, sl = pl.ds(j * ln, ln), slice(g * tw, (g + 1) * tw)
        h = h_scr[g]
        hb = _bf(h)
        y = mm(q, hb) + yc
        h_scr[g] = decay_col * h + mm(m_off, hb) + cc

        mu = _split_dot(y, ones) * (1.0 / RW_N)
        dlt = y - mu
        var = _split_dot(dlt * dlt, ones) * (1.0 / RW_N)
        yn = dlt * lax.rsqrt(var + RW_GN_EPS) * lw_ref[:, sl] + lb_ref[:, sl]
        y_ref[rows, sl] = yn + _split_dot(r_ref[rows, sl] * k_ref[rows, sl] * rk_ref[:, sl], ones) * v_ref[rows, sl]

    @pl.when(ci == pl.num_programs(1) - 1)
    def _():
        for g in range(nt):
            h = h_scr[g]
            for hd in range(RW_TILE_HEADS):
                ho_ref[0, g * RW_TILE_HEADS + hd] = h[hd * RW_N:(hd + 1) * RW_N, hd * RW_N:(hd + 1) * RW_N]


def _rw_scan_chunked(seqs, grp, norm, layer, into):
    blk = RW_CHUNK * RW_CHUNKS_PER_STEP
    assert not grp.has_state and grp.seq % blk == 0 and grp.row0 % blk == 0
    nstep = grp.seq // blk
    base = grp.row0 // blk
    seq_spec = pl.BlockSpec((blk, BR_W), lambda i, c: (i * nstep + c, 0))
    par_spec = pl.BlockSpec((1, BR_W), lambda i, c: (0, 0))
    return _pcall(
        _rw_chunk_kernel, name="rw_chunk_" + grp.name, grid=(grp.batch, nstep),
        in_specs=[seq_spec] * 6 + [par_spec] * 3, args=list(seqs) + list(norm),
        out_specs=[pl.BlockSpec((blk, BR_W), lambda i, c: (base + i * nstep + c, 0)),
                   _layer_block(layer, (1, RW_HEADS, RW_N, RW_N), lambda i, c: (i, 0, 0, 0))],
        out_shape=[jax.ShapeDtypeStruct((N_ROWS, BR_W), F32),
                   jax.ShapeDtypeStruct((DEPTH, grp.batch, RW_HEADS, RW_N, RW_N), F32)],
        into=into, scratch=[pltpu.VMEM((BR_W // RW_TILE, RW_TILE, RW_TILE), F32)],
        semantics=("parallel", "arbitrary"), vmem_mb=48)


MERGE_TM = 256


def _merge_kernel(y0_ref, y1_ref, y2_ref, g_ref, zm_ref, wb_ref, m_ref):
    acc = None
    for n, y_ref in enumerate((y0_ref, y1_ref, y2_ref)):
        ys = y_ref[...] * jax.nn.silu(g_ref[:, n * BR_W:(n + 1) * BR_W])
        proj = jnp.dot(_bf(ys), wb_ref[n], preferred_element_type=F32)
        term = jax.nn.sigmoid(zm_ref[:, n * D_MODEL:(n + 1) * D_MODEL]) * proj
        acc = term if acc is None else acc + term
    m_ref[...] = _bf(acc)


def _merge(ys, z, wb, layer):
    n = z.shape[0]
    yspec = pl.BlockSpec((MERGE_TM, BR_W), lambda i: (i, 0))
    gw, mw = N_BRANCH * BR_W, N_BRANCH * D_MODEL
    return _pcall(
        _merge_kernel, name="merge", grid=(n // MERGE_TM,),
        in_specs=[yspec, yspec, yspec,
                  pl.BlockSpec((MERGE_TM, gw), lambda i: (i, Z_GATE // gw)),
                  pl.BlockSpec((MERGE_TM, mw), lambda i: (i, Z_MERGE // mw)),
                  pl.BlockSpec((None, N_BRANCH, BR_W, D_MODEL), lambda i: (layer, 0, 0, 0),
                               pipeline_mode=pl.Buffered(1))],
        args=[*ys, z, z, wb],
        out_specs=pl.BlockSpec((MERGE_TM, D_MODEL), lambda i: (i, 0)),
        out_shape=jax.ShapeDtypeStruct((n, D_MODEL), BF16),
        semantics=("parallel",), vmem_mb=48)


POST_TM = 256


def _post_kernel(x_ref, m_ref, p_ref, wo_ref, np_ref, pu_ref, pn_ref, pg_ref, o_ref):
    x = x_ref[...] + _rms(jnp.dot(m_ref[...], wo_ref[...], preferred_element_type=F32)) * np_ref[...]
    e = _rms(jnp.dot(_bf(p_ref[...]), pu_ref[...], preferred_element_type=F32)) * pn_ref[...]
    o_ref[...] = x + e * jax.nn.sigmoid(jnp.dot(_bf(x), pg_ref[...], preferred_element_type=F32))


def _post(x, m, p, wo, norm_post, ple_up, ple_norm, ple_gate, layer):
    n = x.shape[0]

    def rows(w):
        return pl.BlockSpec((POST_TM, w), lambda i: (i, 0))

    def const(shape):
        return pl.BlockSpec((None,) + shape, lambda i: (layer, 0, 0), pipeline_mode=pl.Buffered(1))

    return _pcall(
        _post_kernel, name="post", grid=(n // POST_TM,),
        in_specs=[rows(D_MODEL), rows(D_MODEL), _layer_block(layer, (POST_TM, PLE_DIM), lambda i: (i, 0)),
                  const((D_MODEL, D_MODEL)), const((1, D_MODEL)), const((PLE_DIM, D_MODEL)),
                  const((1, D_MODEL)), const((D_MODEL, D_MODEL))],
        args=[x, m, p, wo, norm_post, ple_up, ple_norm, ple_gate],
        out_specs=rows(D_MODEL), out_shape=jax.ShapeDtypeStruct((n, D_MODEL), F32),
        semantics=("parallel",), vmem_mb=48)


def _pad_rows(a, rows):
    return jnp.concatenate([a, jnp.zeros((rows - a.shape[0],) + a.shape[1:], a.dtype)], axis=0)


def _pad_cols(a, cols):
    return jnp.concatenate([a, jnp.zeros(a.shape[:-1] + (cols - a.shape[-1],), a.dtype)], axis=-1)


def _layer_rows(a):
    return a.reshape(a.shape[0], 1, a.shape[1])


def kernel(x_prompt, x_sample, state_hgrn, state_rwkv, state_shift, state_ret, p_prompt, p_sample,
           norm_pre, w_in, hg_lower_bounds, hg_norm, rw_mu, rw_w0, rw_w2, rw_a0, rw_a2, rw_k_k, rw_k_a,
           rw_v0, rw_v1, rw_v2, rw_r_k, rw_ln_w, rw_ln_b, w_branch, w_out, norm_post, ple_up, ple_norm, ple_gate):
    groups = (PROMPT, SAMPLE)
    x = jnp.concatenate([x_prompt.reshape(N_PROMPT, D_MODEL), x_sample.reshape(N_SAMPLE, D_MODEL)], axis=0)
    p = jnp.concatenate([p_prompt.reshape(DEPTH, N_PROMPT, PLE_DIM), p_sample.reshape(DEPTH, N_SAMPLE, PLE_DIM)], axis=1)
    w_packed = _pack_w_in(w_in)
    wb, wo, pu, pg = _bf(w_branch), _bf(w_out), _bf(ple_up), _bf(ple_gate)
    g_pre, g_hg, g_post, g_ple = (_layer_rows(a) for a in (norm_pre, hg_norm, norm_post, ple_norm))
    tables = [_rope_tables(g) for g in groups]
    state_rwkv_t = jnp.transpose(state_rwkv, (0, 2, 3, 4, 1))
    v_first = [None, None]
    shifts = [[], []]
    st_out = [[None, None, None] for _ in groups]

    for l in range(DEPTH):
        z = _inproj(x, g_pre, w_packed, l)
        mu_rkv, mu_lora = _pack_shift_cols(rw_mu[l][None])
        prm = dict(mu_rkv=mu_rkv, mu_lora=mu_lora, w0=rw_w0[l][None],
                   w2=_bf(_pad_rows(rw_w2[l], LORA_PAD)), a0=rw_a0[l][None],
                   a2=_bf(_pad_rows(rw_a2[l], LORA_PAD)), k_k=rw_k_k[l][None], k_a=rw_k_a[l][None])
        if l > 0:
            prm.update(v0=rw_v0[l - 1][None], v1=_bf(_pad_cols(rw_v1[l - 1], LANES)),
                       v2=_bf(_pad_rows(rw_v2[l - 1], LANES)))
        rw_norm = (rw_ln_w[l][None], rw_ln_b[l][None], rw_r_k[l].reshape(1, BR_W))
        ys = [None, None, None]
        for gi, grp in enumerate(groups):
            st_rkv, st_lora = _pack_shift_cols(state_shift[l]) if grp.has_state else (None, None)
            ys[0], st_out[gi][0] = _hgrn(z, grp, l, hg_lower_bounds, g_hg, state_hgrn, (ys[0], st_out[gi][0]))
            *seqs, last_rkv, last_lora = _rw_prep(z, grp, prm, st_rkv, st_lora, v_first[gi])
            shifts[gi].append(_unpack_shift_cols(last_rkv[:, 0], last_lora[:, 0]))
            if l == 0:
                v_first[gi] = seqs[3]
            if grp.has_state:
                ys[1], st_out[gi][1] = _rw_scan(seqs, grp, rw_norm, state_rwkv_t, l, (ys[1], st_out[gi][1]))
            else:
                ys[1], st_out[gi][1] = _rw_scan_chunked(seqs, grp, rw_norm, l, (ys[1], st_out[gi][1]))
            ys[2], st_out[gi][2] = _retention(z, grp, tables[gi], state_ret, l, (ys[2], st_out[gi][2]))
        m = _merge(ys, z, wb, l)
        x = _post(x, m, p, wo, g_post, pu, g_ple, pg, l)

    y_prompt = x[:N_PROMPT].reshape(BATCH, SEQ, D_MODEL)
    y_sample = x[N_PROMPT:].reshape(DEC_BATCH, DEC_SEQ, D_MODEL)
    (hg_p, rw_p, rn_p), (hg_s, rw_s, rn_s) = st_out
    rw_p = jnp.swapaxes(rw_p, -1, -2)
    rw_s = jnp.transpose(rw_s, (0, 4, 1, 2, 3))
    sh_p, sh_s = (jnp.stack(s, axis=0) for s in shifts)
    return (y_prompt, y_sample, hg_p, hg_s, rw_p, rw_s, sh_p, sh_s, rn_p, rn_s)
```

```python
import functools

import numpy as np
import jax
import jax.numpy as jnp
from jax import lax
from jax.experimental import pallas as pl
from jax.experimental.pallas import tpu as pltpu

F32 = jnp.float32
BF16 = jnp.bfloat16
HIGHEST = lax.Precision.HIGHEST

D_MODEL = 2048
BATCH, SEQ = 4, 2048
DEC_BATCH, DEC_SEQ = 128, 8
DEPTH = 2
PAST_LEN = 16384
N_BRANCH = 3
BR_W = 1024
HG_HEADS, HG_DK, HG_DV = 8, 128, 128
F_MIN = 1e-30
RW_HEADS, RW_N = 16, 64
RW_LORA = 96
RW_MV_LORA = 64
RW_GN_EPS = 64e-5
RN_HEADS, RN_DK, RN_DV = 4, 256, 256
ROPE_BASE = 10000.0
CHUNK = 64
PLE_DIM = 256
EPS = 1e-6

LANES = 128
SUBLANES = 8
MXU_DIM = 256

LORA_PAD = LANES
Z_HG = 0
Z_RKV = Z_HG + 3 * BR_W
Z_RN = Z_RKV + 3 * BR_W
Z_GATE = Z_RN + 3 * BR_W
Z_MERGE = Z_GATE + N_BRANCH * BR_W
Z_LORA = Z_MERGE + N_BRANCH * D_MODEL
Z_USED = Z_LORA + 2 * LORA_PAD
INPROJ_TN = 512
Z_W = -(-Z_USED // INPROJ_TN) * INPROJ_TN
INPROJ_TM = 1024

N_PROMPT = BATCH * SEQ
N_SAMPLE = DEC_BATCH * DEC_SEQ
N_ROWS = N_PROMPT + N_SAMPLE


def _bf(x):
    return x.astype(BF16)


def _dot(a, b):
    return jnp.dot(_bf(a), _bf(b), preferred_element_type=F32)


def _dot_nt(a, b):
    return lax.dot_general(_bf(a), _bf(b), (((1,), (1,)), ((), ())), preferred_element_type=F32)


def _dot_tn(a, b):
    return lax.dot_general(_bf(a), _bf(b), (((0,), (0,)), ((), ())), preferred_element_type=F32)


def _split_dot(x, g):
    hi = _bf(x)
    lo = _bf(x - hi.astype(F32))
    return (jnp.dot(hi, g, preferred_element_type=F32) + jnp.dot(lo, g, preferred_element_type=F32))


def _bf_terms(x):
    hi = _bf(x)
    r1 = x - hi.astype(F32)
    mid = _bf(r1)
    return hi, mid, _bf(r1 - mid.astype(F32))


def _sum_dot(m, x):
    m = _bf(m)
    return sum(jnp.dot(m, t, preferred_element_type=F32) for t in _bf_terms(x))


def _sum_dot_tn(x, m):
    m = _bf(m)
    return sum(lax.dot_general(t, m, (((0,), (0,)), ((), ())), preferred_element_type=F32) for t in _bf_terms(x))


def _head_ones(width, head):
    r = lax.broadcasted_iota(jnp.int32, (width, width), 0) // head
    c = lax.broadcasted_iota(jnp.int32, (width, width), 1) // head
    return (r == c).astype(BF16)


def _rms(x):
    return x * lax.rsqrt(jnp.mean(x * x, axis=-1, keepdims=True) + EPS)


def _round_robin(gens):
    live = list(gens)
    while live:
        nxt = []
        for gen in live:
            try:
                next(gen)
                nxt.append(gen)
            except StopIteration:
                pass
        live = nxt


class _Group:
    def __init__(self, name, row0, batch, seq, pos0, has_state):
        self.name, self.row0, self.batch, self.seq, self.pos0, self.has_state = name, row0, batch, seq, pos0, has_state
        self.rows = batch * seq


PROMPT = _Group("prompt", 0, BATCH, SEQ, 0, False)
SAMPLE = _Group("sample", N_PROMPT, DEC_BATCH, DEC_SEQ, PAST_LEN, True)


def _pcall(body, *, name, grid, in_specs, args, out_specs, out_shape, into=None, scratch=(), semantics,
           vmem_mb=None):
    in_specs, args = list(in_specs), list(args)
    n_in, aliases = len(args), {}
    for k, arr in enumerate(into or ()):
        if arr is not None:
            aliases[len(args)] = k
            in_specs.append(pl.BlockSpec(memory_space=pl.ANY))
            args.append(arr)
    n_alias = len(aliases)

    def kernel_fn(*refs):
        return body(*refs[:n_in], *refs[n_in + n_alias:])

    params = dict(dimension_semantics=semantics)
    if vmem_mb is not None:
        params['vmem_limit_bytes'] = vmem_mb << 20
    return pl.pallas_call(
        kernel_fn, grid=grid, in_specs=in_specs, out_specs=out_specs, out_shape=out_shape,
        scratch_shapes=list(scratch), input_output_aliases=aliases,
        compiler_params=pltpu.CompilerParams(**params), name=name)(*args)


def _layer_block(layer, shape, index_map):
    return pl.BlockSpec((None,) + tuple(shape), lambda *g: (layer,) + tuple(index_map(*g)))


D_IN = 3 * BR_W + (3 * BR_W + 2 * RW_LORA) + 3 * BR_W + N_BRANCH * BR_W + N_BRANCH * D_MODEL
PACK_ROWS = 128
PACK_COLS = 2048


def _pack_kernel(w_ref, o_ref):
    o_rw = 3 * BR_W

    def copy(dst, src, n):
        for c in range(0, n, PACK_COLS):
            m = min(PACK_COLS, n - c)
            o_ref[:, dst + c:dst + c + m] = _bf(w_ref[src + c:src + c + m, :].T)

    def lora(src):
        t = w_ref[src:src + LORA_PAD, :].T
        keep = lax.broadcasted_iota(jnp.int32, t.shape, 1) < RW_LORA
        return _bf(jnp.where(keep, t, 0.0))

    copy(Z_HG, 0, o_rw + BR_W)
    copy(Z_RKV + BR_W, o_rw + BR_W + RW_LORA, 2 * BR_W)
    rest = o_rw + 3 * BR_W + 2 * RW_LORA
    copy(Z_RN, rest, D_IN - rest)
    o_ref[:, Z_LORA:Z_LORA + LORA_PAD] = lora(o_rw + BR_W)
    o_ref[:, Z_LORA + LORA_PAD:Z_USED] = lora(rest - RW_LORA)
    o_ref[:, Z_USED:] = jnp.zeros((PACK_ROWS, Z_W - Z_USED), BF16)


def _pack_w_in(w_in):
    return _pcall(
        _pack_kernel, name="pack_w_in", grid=(DEPTH, D_MODEL // PACK_ROWS),
        in_specs=[pl.BlockSpec((None, D_IN, PACK_ROWS), lambda l, i: (l, 0, i))], args=[jnp.swapaxes(w_in, 1, 2)],
        out_specs=pl.BlockSpec((None, PACK_ROWS, Z_W), lambda l, i: (l, i, 0)),
        out_shape=jax.ShapeDtypeStruct((DEPTH, D_MODEL, Z_W), BF16),
        semantics=("parallel", "parallel"), vmem_mb=48)


def _pack_shift_cols(a):
    r = a[..., :BR_W]
    wlo = a[..., BR_W:BR_W + RW_LORA]
    k = a[..., BR_W + RW_LORA:2 * BR_W + RW_LORA]
    v = a[..., 2 * BR_W + RW_LORA:3 * BR_W + RW_LORA]
    alo = a[..., 3 * BR_W + RW_LORA:]
    zl = jnp.zeros(a.shape[:-1] + (LORA_PAD - RW_LORA,), a.dtype)
    return jnp.concatenate([r, k, v], axis=-1), jnp.concatenate([wlo, zl, alo, zl], axis=-1)


def _unpack_shift_cols(rkv, lora):
    return jnp.concatenate([rkv[:, :BR_W], lora[:, :RW_LORA], rkv[:, BR_W:],
                            lora[:, LORA_PAD:LORA_PAD + RW_LORA]], axis=1)


def _inproj_kernel(x_ref, g_ref, w_ref, o_ref, h_ref):
    @pl.when(pl.program_id(1) == 0)
    def _():
        h_ref[...] = _bf(_rms(x_ref[...]) * g_ref[...])

    o_ref[...] = jnp.dot(h_ref[...], w_ref[...], preferred_element_type=F32)


def _inproj(x, g, w, layer):
    n = x.shape[0]
    return _pcall(
        _inproj_kernel, name="inproj", grid=(n // INPROJ_TM, Z_W // INPROJ_TN),
        in_specs=[pl.BlockSpec((INPROJ_TM, D_MODEL), lambda i, j: (i, 0)),
                  _layer_block(layer, (1, D_MODEL), lambda i, j: (0, 0)),
                  _layer_block(layer, (D_MODEL, INPROJ_TN), lambda i, j: (0, j))],
        args=[x, g, w],
        out_specs=pl.BlockSpec((INPROJ_TM, INPROJ_TN), lambda i, j: (i, j)),
        out_shape=jax.ShapeDtypeStruct((n, Z_W), F32),
        scratch=[pltpu.VMEM((INPROJ_TM, D_MODEL), BF16)],
        semantics=("parallel", "arbitrary"), vmem_mb=48)


def _gla_level_matrix(c):
    nlv = c.bit_length() - 1
    m = np.zeros(((nlv + 1) * c, c), np.float32)
    for r in range(c):
        m[r, :r + 1] = 1.0
        for p in range(nlv):
            bd = ((r >> (p + 1)) << (p + 1)) + (1 << p) - 1
            if (r >> p) & 1:
                m[(p + 1) * c + r, bd + 1:r + 1] = 1.0
            else:
                m[(p + 1) * c + r, r + 1:bd + 1] = 1.0
    return m


def _hgrn_kernel(*refs, c, nb, layer, has_state):
    if has_state:
        hq_ref, hf_ref, hi_ref, lbp_ref, gn_ref, ms_ref, s0_ref, y_ref, so_ref, s_scr = refs
    else:
        hq_ref, hf_ref, hi_ref, lbp_ref, gn_ref, ms_ref, y_ref, so_ref, s_scr = refs
    nlv = c.bit_length() - 1
    ci = pl.program_id(1)

    @pl.when(ci == 0)
    def _():
        if has_state:
            s_scr[...] = s0_ref[...]
        else:
            s_scr[...] = jnp.zeros_like(s_scr)

    lbp = lbp_ref[...]
    e = jnp.exp(lbp - jnp.max(lbp, axis=0, keepdims=True))
    sm = e / jnp.sum(e, axis=0, keepdims=True)
    lb = jnp.sum(sm[0:layer + 1], axis=0, keepdims=True) - sm[0:1]

    ti = lax.broadcasted_iota(jnp.int32, (c, c), 0)
    si = lax.broadcasted_iota(jnp.int32, (c, c), 1)
    txs = ti ^ si
    lower = ti > si
    ones_c = jnp.ones((c, LANES), F32)

    for ib in range(nb):
        rows = pl.ds(ib * c, c)
        hq, hf, v = hq_ref[rows, :], hf_ref[rows, :], hi_ref[rows, :]
        sig = jax.nn.sigmoid(hf)
        f_gate = lb + (1.0 - lb) * sig
        logf = jnp.log(jnp.maximum(f_gate, F_MIN))
        k = (1.0 - lb) * (1.0 - sig)
        q = jax.nn.silu(hq) * HG_DK ** -0.5
        lv = _sum_dot(ms_ref[...], logf)
        bl_col = _sum_dot_tn(logf, ones_c)
        qk = q * k

        def head(h):
            hs = slice(h * HG_DK, (h + 1) * HG_DK)
            qh, kh, vh, b = q[:, hs], k[:, hs], v[:, hs], lv[0:c, hs]
            att = jnp.where(ti == si, jnp.sum(qk[:, hs], axis=-1, keepdims=True), 0.0)
            for p in range(nlv):
                ex = jnp.exp(lv[(p + 1) * c:(p + 2) * c, hs])
                att = att + jnp.where(((txs >> p) == 1) & lower, _dot_nt(qh * ex, kh * ex), 0.0)
            yield
            s = s_scr[ib, h]
            o = _dot(att, vh) + _dot(qh * jnp.exp(b), s)
            bl = b[c - 1:c]
            s_scr[ib, h] = jnp.exp(bl_col[hs, :]) * s + _dot_tn(kh * jnp.exp(bl - b), vh)
            yield
            y_ref[rows, hs] = _rms(o) * gn_ref[...]

        _round_robin([head(h) for h in range(HG_HEADS)])

    @pl.when(ci == pl.num_programs(1) - 1)
    def _():
        so_ref[...] = s_scr[...]


def _hgrn(z, grp, layer, lower_bounds, gnorm, state, into):
    c = min(CHUNK, grp.seq)
    nb = 1 if grp.seq > c else 2
    nchunk = grp.seq // c
    blk = nb * c
    base = grp.row0 // blk

    def zspec(col0):
        return pl.BlockSpec((blk, BR_W), lambda i, ci: (base + i * nchunk + ci, col0 // BR_W))

    in_specs = [zspec(Z_HG), zspec(Z_HG + BR_W), zspec(Z_HG + 2 * BR_W),
                pl.BlockSpec((DEPTH, BR_W), lambda i, ci: (0, 0)),
                _layer_block(layer, (1, HG_DV), lambda i, ci: (0, 0)),
                pl.BlockSpec(((c.bit_length()) * c, c), lambda i, ci: (0, 0))]
    args = [z, z, z, lower_bounds, gnorm, jnp.asarray(_gla_level_matrix(c))]
    st_spec = _layer_block(layer, (nb, HG_HEADS, HG_DK, HG_DV), lambda i, ci: (i, 0, 0, 0))
    if grp.has_state:
        in_specs.append(st_spec)
        args.append(state)
    return _pcall(
        functools.partial(_hgrn_kernel, c=c, nb=nb, layer=layer, has_state=grp.has_state),
        name="hgrn_" + grp.name, grid=(grp.batch // nb, nchunk), in_specs=in_specs, args=args,
        out_specs=[pl.BlockSpec((blk, BR_W), lambda i, ci: (base + i * nchunk + ci, 0)), st_spec],
        out_shape=[jax.ShapeDtypeStruct((N_ROWS, BR_W), F32),
                   jax.ShapeDtypeStruct((DEPTH, grp.batch, HG_HEADS, HG_DK, HG_DV), F32)],
        into=into, scratch=[pltpu.VMEM((nb, HG_HEADS, HG_DK, HG_DV), F32)],
        semantics=("parallel", "arbitrary"), vmem_mb=48)


def _rope_kernel(inv_ref, cos_ref, sin_ref, *, pos0):
    t = cos_ref.shape[0]
    pos = (pos0 + lax.broadcasted_iota(jnp.int32, (t, RN_DK // 2), 0)).astype(F32)
    ang = pos * inv_ref[...]
    cos_ref[...] = jnp.cos(ang)
    sin_ref[...] = jnp.sin(ang)


def _rope_tables(grp):
    inv = ROPE_BASE ** (-jnp.arange(0, RN_DK, 2, dtype=F32) / RN_DK)
    shp = jax.ShapeDtypeStruct((grp.seq, RN_DK // 2), F32)
    return pl.pallas_call(functools.partial(_rope_kernel, pos0=grp.pos0), out_shape=[shp, shp],
                          name="rope")(inv[None, :])


def _ret_kernel(*refs, c, nb, has_state):
    if has_state:
        (q_ref, k_ref, v_ref, cos_ref, sin_ref, dm_ref, qd_ref, kd_ref, cd_ref, s0_ref,
         y_ref, so_ref, s_scr) = refs
    else:
        (q_ref, k_ref, v_ref, cos_ref, sin_ref, dm_ref, qd_ref, kd_ref, cd_ref,
         y_ref, so_ref, s_scr) = refs
    ci = pl.program_id(1)

    @pl.when(ci == 0)
    def _():
        if has_state:
            s_scr[...] = s0_ref[...]
        else:
            s_scr[...] = jnp.zeros_like(s_scr)

    cos, sin = cos_ref[...], sin_ref[...]
    half = RN_DK // 2

    def rot(x):
        x1, x2 = x[:, :half], x[:, half:]
        return jnp.concatenate([x1 * cos - x2 * sin, x1 * sin + x2 * cos], axis=1)

    def head(ib, h):
        rows, hs = pl.ds(ib * c, c), slice(h * RN_DK, (h + 1) * RN_DK)
        q = rot(q_ref[rows, hs])
        k = rot(k_ref[rows, hs]) * RN_DK ** -0.5
        v = v_ref[rows, hs]
        s = s_scr[ib, h]
        att = _dot_nt(q, k) * dm_ref[h]
        qs = _dot(q, s)
        s_scr[ib, h] = cd_ref[h] * s + _dot_tn(k * kd_ref[h], v)
        yield
        o = _dot(att, v) + qs * qd_ref[h]
        yield
        y_ref[rows, hs] = _rms(o)

    _round_robin([head(ib, h) for ib in range(nb) for h in range(RN_HEADS)])

    @pl.when(ci == pl.num_programs(1) - 1)
    def _():
        so_ref[...] = s_scr[...]


def _retention(z, grp, tables, state, layer, into):
    c = min(CHUNK, grp.seq)
    nb = 1 if grp.seq > c else 2
    nchunk = grp.seq // c
    blk = nb * c
    base = grp.row0 // blk
    log_gamma = jnp.log(1.0 - 2.0 ** (-5.0 - jnp.arange(RN_HEADS, dtype=F32)))
    j = jnp.arange(c, dtype=F32)
    rel = j[:, None] - j[None, :]
    dmat = jnp.where(rel >= 0, jnp.exp(log_gamma[:, None, None] * jnp.maximum(rel, 0.0)), 0.0)
    q_dec = jnp.exp(log_gamma[:, None] * (j + 1.0))[..., None]
    k_dec = jnp.exp(log_gamma[:, None] * (c - 1.0 - j))[..., None]
    c_dec = jnp.exp(log_gamma * c)[:, None, None]

    def zspec(col0):
        return pl.BlockSpec((blk, BR_W), lambda i, ci: (base + i * nchunk + ci, col0 // BR_W))

    def const(shape):
        return pl.BlockSpec(shape, lambda i, ci: (0,) * len(shape))

    tspec = pl.BlockSpec((c, RN_DK // 2), lambda i, ci: (ci, 0))
    in_specs = [zspec(Z_RN), zspec(Z_RN + BR_W), zspec(Z_RN + 2 * BR_W), tspec, tspec,
                const((RN_HEADS, c, c)), const((RN_HEADS, c, 1)), const((RN_HEADS, c, 1)), const((RN_HEADS, 1, 1))]
    args = [z, z, z, tables[0], tables[1], dmat, q_dec, k_dec, c_dec]
    st_spec = _layer_block(layer, (nb, RN_HEADS, RN_DK, RN_DV), lambda i, ci: (i, 0, 0, 0))
    if grp.has_state:
        in_specs.append(st_spec)
        args.append(state)
    return _pcall(
        functools.partial(_ret_kernel, c=c, nb=nb, has_state=grp.has_state),
        name="ret_" + grp.name, grid=(grp.batch // nb, nchunk), in_specs=in_specs, args=args,
        out_specs=[pl.BlockSpec((blk, BR_W), lambda i, ci: (base + i * nchunk + ci, 0)), st_spec],
        out_shape=[jax.ShapeDtypeStruct((N_ROWS, BR_W), F32),
                   jax.ShapeDtypeStruct((DEPTH, grp.batch, RN_HEADS, RN_DK, RN_DV), F32)],
        into=into, scratch=[pltpu.VMEM((nb, RN_HEADS, RN_DK, RN_DV), F32)],
        semantics=("parallel", "arbitrary"), vmem_mb=48)


RW_PREP_ROWS = 256


def _rw_prep_kernel(*refs, tt, nb, has_state, with_vmix):
    refs = list(refs)
    zr_ref, zl_ref = refs[:2]
    pos = 2
    if has_state:
        sr_ref, sl_ref = refs[pos:pos + 2]
        pos += 2
    (mur_ref, mul_ref, w0_ref, w2_ref, a0_ref, a2_ref, kk_ref, ka_ref) = refs[pos:pos + 8]
    pos += 8
    if with_vmix:
        v0_ref, v1_ref, v2_ref, vf_ref = refs[pos:pos + 4]
        pos += 4
    r_ref, d_ref, k_ref, v_ref, a_ref, b_ref, lastr_ref, lastl_ref = refs[pos:pos + 8]
    pr_scr, pl_scr = refs[pos + 8:pos + 10]
    ti = pl.program_id(1)
    rows = nb * tt

    zr, zl = zr_ref[...], zl_ref[...]
    for ib in range(nb):
        lastr_ref[ib] = zr_ref[pl.ds(ib * tt + tt - 1, 1), :]
        lastl_ref[ib] = zl_ref[pl.ds(ib * tt + tt - 1, 1), :]
    first = (lax.broadcasted_iota(jnp.int32, (rows, 1), 0) % tt) == 0

    def prev_rows(z, st_ref, scr):
        w = z.shape[1]
        if nb > 1:
            st = st_ref[...]
            return jnp.broadcast_to(st[:, None, :], (nb, tt, w)).reshape(rows, w)

        @pl.when(ti == 0)
        def _():
            scr[0:1, :] = st_ref[...] if has_state else jnp.zeros((1, w), F32)

        return scr[0:1, :]

    def shifted(z, st_ref, scr, mu):
        prev = jnp.where(first, prev_rows(z, st_ref, scr), pltpu.roll(z, 1, 0))
        scr[0:1, :] = z[rows - 1:rows, :]
        return z + (prev - z) * mu

    ur = shifted(zr, sr_ref if has_state else None, pr_scr, mur_ref[...])
    ul = shifted(zl, sl_ref if has_state else None, pl_scr, mul_ref[...])

    r, k, v = ur[:, :BR_W], ur[:, BR_W:2 * BR_W], ur[:, 2 * BR_W:]
    wlo, alo = ul[:, :LORA_PAD], ul[:, LORA_PAD:]
    w = -jax.nn.softplus(-(w0_ref[...] + _dot(jnp.tanh(wlo), w2_ref[...]))) - 0.5
    a = jax.nn.sigmoid(a0_ref[...] + _dot(alo, a2_ref[...]))
    kk = k * kk_ref[...]
    ones = _head_ones(MXU_DIM, RW_N)
    kk2 = kk * kk
    nrm2 = jnp.concatenate(
        [_split_dot(kk2[:, g * MXU_DIM:(g + 1) * MXU_DIM], ones) for g in range(BR_W // MXU_DIM)], axis=1)
    kk = kk / jnp.maximum(jnp.sqrt(nrm2), 1e-12)
    k = k * (1.0 + (a - 1.0) * ka_ref[...])
    if with_vmix:
        mix = jax.nn.sigmoid(v0_ref[...] + _dot(_dot(v, v1_ref[...]), v2_ref[...]))
        v = v + (vf_ref[...] - v) * mix
    r_ref[...] = r
    d_ref[...] = -jnp.exp(w)
    k_ref[...] = k
    v_ref[...] = v
    a_ref[...] = -kk
    b_ref[...] = kk * a


def _rw_prep(z, grp, prm, state_rkv, state_lora, v_first):
    tt = min(RW_PREP_ROWS, grp.seq)
    nb = RW_PREP_ROWS // tt
    ntile = grp.seq // tt
    rows = nb * tt
    base = grp.row0 // rows
    with_vmix = v_first is not None
    w_rkv, w_lora = 3 * BR_W, 2 * LORA_PAD

    def rowmap(i, t):
        return base + i * ntile + t

    def const(shape):
        return pl.BlockSpec(shape, lambda i, t: (0,) * len(shape))

    in_specs = [pl.BlockSpec((rows, w_rkv), lambda i, t: (rowmap(i, t), Z_RKV // w_rkv)),
                pl.BlockSpec((rows, w_lora), lambda i, t: (rowmap(i, t), Z_LORA // w_lora))]
    args = [z, z]
    if grp.has_state:
        in_specs += [pl.BlockSpec((nb, w_rkv), lambda i, t: (i, 0)),
                     pl.BlockSpec((nb, w_lora), lambda i, t: (i, 0))]
        args += [state_rkv, state_lora]
    in_specs += [const((1, w_rkv)), const((1, w_lora)), const((1, BR_W)), const((LORA_PAD, BR_W)),
                 const((1, BR_W)), const((LORA_PAD, BR_W)), const((1, BR_W)), const((1, BR_W))]
    args += [prm['mu_rkv'], prm['mu_lora'], prm['w0'], prm['w2'], prm['a0'], prm['a2'], prm['k_k'], prm['k_a']]
    out_spec = pl.BlockSpec((rows, BR_W), lambda i, t: (i * ntile + t, 0))
    if with_vmix:
        in_specs += [const((1, BR_W)), const((BR_W, LANES)), const((LANES, BR_W)), out_spec]
        args += [prm['v0'], prm['v1'], prm['v2'], v_first]
    out = jax.ShapeDtypeStruct((grp.rows, BR_W), F32)

    def last_spec(w):
        return pl.BlockSpec((nb, 1, w), lambda i, t: (i, 0, 0))

    def last_shape(w):
        return jax.ShapeDtypeStruct((grp.batch, 1, w), F32)

    return pl.pallas_call(
        functools.partial(_rw_prep_kernel, tt=tt, nb=nb, has_state=grp.has_state, with_vmix=with_vmix),
        grid=(grp.batch // nb, ntile),
        in_specs=in_specs,
        out_specs=[out_spec] * 6 + [last_spec(w_rkv), last_spec(w_lora)],
        out_shape=[out] * 6 + [last_shape(w_rkv), last_shape(w_lora)],
        scratch_shapes=[pltpu.VMEM((SUBLANES, w_rkv), F32), pltpu.VMEM((SUBLANES, w_lora), F32)],
        compiler_params=pltpu.CompilerParams(
            dimension_semantics=("parallel", "arbitrary"), vmem_limit_bytes=48 << 20),
        name="rw_prep_" + ("sample" if grp.has_state else "prompt"),
    )(*args)


RW_TILE = MXU_DIM
RW_TILE_HEADS = RW_TILE // RW_N
RW_LANE_HEADS = LANES // RW_N
RW_SCAN_UNROLL = 4


def _rw_scan_kernel(r_ref, d_ref, k_ref, v_ref, a_ref, b_ref, lw_ref, lb_ref, rk_ref, s0_ref,
                    y_ref, so_ref, s_scr, v_scr, y_scr):
    t = pl.program_id(1)

    @pl.when(t == 0)
    def _():
        s_scr[...] = s0_ref[...]

    r, ld, k, a, b = (ref[...].T for ref in (r_ref, d_ref, k_ref, a_ref, b_ref))
    v_scr[...] = v_ref[...].T
    w = jnp.exp(ld)

    for h in range(RW_LANE_HEADS):
        hs = slice(h * RW_N, (h + 1) * RW_N)
        rh, wh, kh, ah, bh = r[hs], w[hs], k[hs], a[hs], b[hs]

        def row_update(i, carry, h=h, rh=rh, wh=wh, kh=kh, ah=ah, bh=bh):
            s = s_scr[h, i]
            sa = jnp.sum(s * ah, axis=0, keepdims=True)
            s = s * wh + sa * bh + v_scr[pl.ds(h * RW_N + i, 1), :] * kh
            s_scr[h, i] = s
            y_scr[pl.ds(h * RW_N + i, 1), :] = jnp.sum(s * rh, axis=0, keepdims=True)
            return carry

        lax.fori_loop(0, RW_N, row_update, 0, unroll=RW_SCAN_UNROLL)

        y, vh = y_scr[hs, :], v_scr[hs, :]
        mu = jnp.mean(y, axis=0, keepdims=True)
        dlt = y - mu
        var = jnp.mean(dlt * dlt, axis=0, keepdims=True)
        yn = dlt * lax.rsqrt(var + RW_GN_EPS) * lw_ref[h] + lb_ref[h]
        bonus = jnp.sum(rh * kh * rk_ref[h], axis=0, keepdims=True)
        y_scr[hs, :] = yn + bonus * vh

    y_ref[...] = y_scr[...].T

    @pl.when(t == pl.num_programs(1) - 1)
    def _():
        so_ref[...] = s_scr[...]


def _rw_scan(seqs, grp, norm, state_t, layer, into):
    assert grp.has_state and grp.batch == LANES
    hp = RW_LANE_HEADS
    width = hp * RW_N
    tm = jnp.stack(seqs).reshape(len(seqs), grp.batch, grp.seq, BR_W).transpose(0, 2, 1, 3)
    seq_specs = [pl.BlockSpec((None, None, grp.batch, width), lambda j, t, n=n: (n, t, 0, j))
                 for n in range(len(seqs))]
    par_spec = pl.BlockSpec((hp, RW_N, 1), lambda j, t: (j, 0, 0))
    st_spec = _layer_block(layer, (hp, RW_N, RW_N, grp.batch), lambda j, t: (j, 0, 0, 0))
    y, st = _pcall(
        _rw_scan_kernel, name="rw_scan_" + grp.name, grid=(RW_HEADS // hp, grp.seq),
        in_specs=seq_specs + [par_spec] * 3 + [st_spec],
        args=[tm] * len(seqs) + [p.reshape(RW_HEADS, RW_N, 1) for p in norm] + [state_t],
        out_specs=[pl.BlockSpec((None, grp.batch, width), lambda j, t: (t, 0, j)), st_spec],
        out_shape=[jax.ShapeDtypeStruct((grp.seq, grp.batch, BR_W), F32),
                   jax.ShapeDtypeStruct((DEPTH, RW_HEADS, RW_N, RW_N, grp.batch), F32)],
        into=(None, into[1]),
        scratch=[pltpu.VMEM((hp, RW_N, RW_N, grp.batch), F32), pltpu.VMEM((width, grp.batch), F32),
                 pltpu.VMEM((width, grp.batch), F32)],
        semantics=("parallel", "arbitrary"), vmem_mb=48)
    y_rows = y.transpose(1, 0, 2).reshape(grp.rows, BR_W)
    return lax.dynamic_update_slice(into[0], y_rows, (grp.row0, 0)), st


RW_CHUNK = RW_N
RW_CHUNKS_PER_STEP = 2
RW_SOLVE_BASE = 8


def _rw_chunk_kernel(r_ref, d_ref, k_ref, v_ref, a_ref, b_ref, lw_ref, lb_ref, rk_ref, y_ref, ho_ref, h_scr):
    ci = pl.program_id(1)
    ln, tw, nt = RW_CHUNK, RW_TILE, BR_W // RW_TILE

    @pl.when(ci == 0)
    def _():
        h_scr[...] = jnp.zeros_like(h_scr)

    row = lax.broadcasted_iota(jnp.int32, (tw, tw), 0)
    col = lax.broadcasted_iota(jnp.int32, (tw, tw), 1)
    same_head = (row // RW_N) == (col // RW_N)
    ones = same_head.astype(BF16)
    t_i = lax.broadcasted_iota(jnp.int32, (ln, tw), 0)
    s_i = lax.broadcasted_iota(jnp.int32, (ln, tw), 1) % ln
    strict, incl = s_i < t_i, s_i <= t_i
    tril = (lax.broadcasted_iota(jnp.int32, (ln, ln), 0) >= lax.broadcasted_iota(jnp.int32, (ln, ln), 1)).astype(F32)

    def blocks(x):
        return jnp.where(same_head, jnp.concatenate([x] * RW_TILE_HEADS, axis=0), 0.0).astype(BF16)

    def mm(x, w):
        return jnp.dot(_bf(x), w, preferred_element_type=F32)

    def prepare(j, g, out):
        rows, sl = pl.ds(j * ln, ln), slice(g * tw, (g + 1) * tw)
        r, ld, k, v, a, b = (ref[rows, sl] for ref in (r_ref, d_ref, k_ref, v_ref, a_ref, b_ref))
        c = _sum_dot(tril, ld)
        tot_col = _sum_dot_tn(ld, jnp.ones((ln, LANES), F32))
        yield
        c_last = c[ln - 1:ln]
        e_inv, e_end = jnp.exp(-c), jnp.exp(c_last - c)
        at, rt = a * jnp.exp(c - ld), r * jnp.exp(c)
        bt, kt, bp, kp = b * e_inv, k * e_inv, b * e_end, k * e_end
        gram = lax.dot_general(_bf(jnp.concatenate([at, rt], axis=0)),
                               jnp.concatenate([blocks(bt), blocks(kt)], axis=0),
                               (((1,), (1,)), ((), ())), preferred_element_type=F32)
        yield
        n_ab = jnp.where(strict, gram[:ln, :tw], 0.0)
        a_ak = jnp.where(strict, gram[:ln, tw:], 0.0)
        a_rb = jnp.where(incl, gram[ln:, :tw], 0.0)
        a_rk = jnp.where(incl, gram[ln:, tw:], 0.0)
        av = mm(jnp.concatenate([a_ak, a_rk], axis=0), blocks(v))
        yield
        def same_block(m):
            return (t_i // m) == (s_i // m)

        nd = jnp.where(same_block(RW_SOLVE_BASE), n_ab, 0.0)
        tinv = jnp.where(s_i == t_i, 1.0, 0.0) + nd
        p = mm(nd, blocks(nd))
        yield
        m = 2
        while m < RW_SOLVE_BASE:
            m *= 2
            if m < RW_SOLVE_BASE:
                both = mm(jnp.concatenate([tinv, p], axis=0), blocks(p))
                tinv, p = tinv + both[:ln], both[ln:]
            else:
                tinv = tinv + mm(tinv, blocks(p))
            yield
        m = RW_SOLVE_BASE
        while m < ln:
            lower_left = jnp.where(same_block(2 * m) & jnp.logical_not(same_block(m)), n_ab, 0.0)
            tn = mm(tinv, blocks(lower_left))
            yield
            tinv = tinv + mm(tn, blocks(tinv))
            yield
            m *= 2
        w = mm(tinv, jnp.concatenate([blocks(at), blocks(av[:ln])], axis=1))
        yield
        at2, uv = w[:, :tw], w[:, tw:]
        m_off = jnp.where(same_head, _dot_tn(bp, at2), 0.0)
        cc = jnp.where(same_head, _dot_tn(jnp.concatenate([bp, kp], axis=0), jnp.concatenate([uv, v], axis=0)), 0.0)
        qy = mm(a_rb, jnp.concatenate([blocks(at2), blocks(uv)], axis=1))
        decay_col = jnp.concatenate([jnp.exp(tot_col)] * (tw // LANES), axis=1)
        out[j, g] = (rt + qy[:, :tw], qy[:, tw:] + av[ln:], m_off, cc, decay_col)

    items = [(j, g) for j in range(RW_CHUNKS_PER_STEP) for g in range(nt)]
    prepared = {}
    _round_robin([prepare(j, g, prepared) for j, g in items])

    for j, g in items:
        q, yc, m_off, cc, decay_col = prepared[(j, g)]
        rows, sl = pl.ds(j * ln, ln), slice(g * tw, (g + 1) * tw)
        h = h_scr[g]
        qm = mm(jnp.concatenate([q, m_off], axis=0), _bf(h))
        y = qm[:ln] + yc
        h_scr[g] = decay_col * h + qm[ln:] + cc

        mu = _split_dot(y, ones) * (1.0 / RW_N)
        dlt = y - mu
        var = _split_dot(dlt * dlt, ones) * (1.0 / RW_N)
        yn = dlt * lax.rsqrt(var + RW_GN_EPS) * lw_ref[:, sl] + lb_ref[:, sl]
        y_ref[rows, sl] = yn + _split_dot(r_ref[rows, sl] * k_ref[rows, sl] * rk_ref[:, sl], ones) * v_ref[rows, sl]

    @pl.when(ci == pl.num_programs(1) - 1)
    def _():
        for g in range(nt):
            h = h_scr[g]
            for hd in range(RW_TILE_HEADS):
                ho_ref[0, g * RW_TILE_HEADS + hd] = h[hd * RW_N:(hd + 1) * RW_N, hd * RW_N:(hd + 1) * RW_N]


def _rw_scan_chunked(seqs, grp, norm, layer, into):
    blk = RW_CHUNK * RW_CHUNKS_PER_STEP
    assert not grp.has_state and grp.seq % blk == 0 and grp.row0 % blk == 0
    nstep = grp.seq // blk
    base = grp.row0 // blk
    seq_spec = pl.BlockSpec((blk, BR_W), lambda i, c: (i * nstep + c, 0))
    par_spec = pl.BlockSpec((1, BR_W), lambda i, c: (0, 0))
    return _pcall(
        _rw_chunk_kernel, name="rw_chunk_" + grp.name, grid=(grp.batch, nstep),
        in_specs=[seq_spec] * 6 + [par_spec] * 3, args=list(seqs) + list(norm),
        out_specs=[pl.BlockSpec((blk, BR_W), lambda i, c: (base + i * nstep + c, 0)),
                   _layer_block(layer, (1, RW_HEADS, RW_N, RW_N), lambda i, c: (i, 0, 0, 0))],
        out_shape=[jax.ShapeDtypeStruct((N_ROWS, BR_W), F32),
                   jax.ShapeDtypeStruct((DEPTH, grp.batch, RW_HEADS, RW_N, RW_N), F32)],
        into=into, scratch=[pltpu.VMEM((BR_W // RW_TILE, RW_TILE, RW_TILE), F32)],
        semantics=("parallel", "arbitrary"), vmem_mb=48)


MERGE_TM = 256


def _merge_kernel(y0_ref, y1_ref, y2_ref, g_ref, zm_ref, wb_ref, m_ref):
    acc = None
    for n, y_ref in enumerate((y0_ref, y1_ref, y2_ref)):
        ys = y_ref[...] * jax.nn.silu(g_ref[:, n * BR_W:(n + 1) * BR_W])
        proj = jnp.dot(_bf(ys), wb_ref[n], preferred_element_type=F32)
        term = jax.nn.sigmoid(zm_ref[:, n * D_MODEL:(n + 1) * D_MODEL]) * proj
        acc = term if acc is None else acc + term
    m_ref[...] = _bf(acc)


def _merge(ys, z, wb, layer):
    n = z.shape[0]
    yspec = pl.BlockSpec((MERGE_TM, BR_W), lambda i: (i, 0))
    gw, mw = N_BRANCH * BR_W, N_BRANCH * D_MODEL
    return _pcall(
        _merge_kernel, name="merge", grid=(n // MERGE_TM,),
        in_specs=[yspec, yspec, yspec,
                  pl.BlockSpec((MERGE_TM, gw), lambda i: (i, Z_GATE // gw)),
                  pl.BlockSpec((MERGE_TM, mw), lambda i: (i, Z_MERGE // mw)),
                  pl.BlockSpec((None, N_BRANCH, BR_W, D_MODEL), lambda i: (layer, 0, 0, 0),
                               pipeline_mode=pl.Buffered(1))],
        args=[*ys, z, z, wb],
        out_specs=pl.BlockSpec((MERGE_TM, D_MODEL), lambda i: (i, 0)),
        out_shape=jax.ShapeDtypeStruct((n, D_MODEL), BF16),
        semantics=("parallel",), vmem_mb=48)


POST_TM = 256


def _post_kernel(x_ref, m_ref, p_ref, wo_ref, np_ref, pu_ref, pn_ref, pg_ref, o_ref):
    x = x_ref[...] + _rms(jnp.dot(m_ref[...], wo_ref[...], preferred_element_type=F32)) * np_ref[...]
    e = _rms(jnp.dot(_bf(p_ref[...]), pu_ref[...], preferred_element_type=F32)) * pn_ref[...]
    o_ref[...] = x + e * jax.nn.sigmoid(jnp.dot(_bf(x), pg_ref[...], preferred_element_type=F32))


def _post(x, m, p, wo, norm_post, ple_up, ple_norm, ple_gate, layer):
    n = x.shape[0]

    def rows(w):
        return pl.BlockSpec((POST_TM, w), lambda i: (i, 0))

    def const(shape):
        return pl.BlockSpec((None,) + shape, lambda i: (layer, 0, 0), pipeline_mode=pl.Buffered(1))

    return _pcall(
        _post_kernel, name="post", grid=(n // POST_TM,),
        in_specs=[rows(D_MODEL), rows(D_MODEL), _layer_block(layer, (POST_TM, PLE_DIM), lambda i: (i, 0)),
                  const((D_MODEL, D_MODEL)), const((1, D_MODEL)), const((PLE_DIM, D_MODEL)),
                  const((1, D_MODEL)), const((D_MODEL, D_MODEL))],
        args=[x, m, p, wo, norm_post, ple_up, ple_norm, ple_gate],
        out_specs=rows(D_MODEL), out_shape=jax.ShapeDtypeStruct((n, D_MODEL), F32),
        semantics=("parallel",), vmem_mb=48)


def _pad_rows(a, rows):
    return jnp.concatenate([a, jnp.zeros((rows - a.shape[0],) + a.shape[1:], a.dtype)], axis=0)


def _pad_cols(a, cols):
    return jnp.concatenate([a, jnp.zeros(a.shape[:-1] + (cols - a.shape[-1],), a.dtype)], axis=-1)


def _layer_rows(a):
    return a.reshape(a.shape[0], 1, a.shape[1])


def kernel(x_prompt, x_sample, state_hgrn, state_rwkv, state_shift, state_ret, p_prompt, p_sample,
           norm_pre, w_in, hg_lower_bounds, hg_norm, rw_mu, rw_w0, rw_w2, rw_a0, rw_a2, rw_k_k, rw_k_a,
           rw_v0, rw_v1, rw_v2, rw_r_k, rw_ln_w, rw_ln_b, w_branch, w_out, norm_post, ple_up, ple_norm, ple_gate):
    groups = (PROMPT, SAMPLE)
    x = jnp.concatenate([x_prompt.reshape(N_PROMPT, D_MODEL), x_sample.reshape(N_SAMPLE, D_MODEL)], axis=0)
    p = jnp.concatenate([p_prompt.reshape(DEPTH, N_PROMPT, PLE_DIM), p_sample.reshape(DEPTH, N_SAMPLE, PLE_DIM)], axis=1)
    w_packed = _pack_w_in(w_in)
    wb, wo, pu, pg = _bf(w_branch), _bf(w_out), _bf(ple_up), _bf(ple_gate)
    g_pre, g_hg, g_post, g_ple = (_layer_rows(a) for a in (norm_pre, hg_norm, norm_post, ple_norm))
    tables = [_rope_tables(g) for g in groups]
    state_rwkv_t = jnp.transpose(state_rwkv, (0, 2, 3, 4, 1))
    v_first = [None, None]
    shifts = [[], []]
    st_out = [[None, None, None] for _ in groups]

    for l in range(DEPTH):
        z = _inproj(x, g_pre, w_packed, l)
        mu_rkv, mu_lora = _pack_shift_cols(rw_mu[l][None])
        prm = dict(mu_rkv=mu_rkv, mu_lora=mu_lora, w0=rw_w0[l][None],
                   w2=_bf(_pad_rows(rw_w2[l], LORA_PAD)), a0=rw_a0[l][None],
                   a2=_bf(_pad_rows(rw_a2[l], LORA_PAD)), k_k=rw_k_k[l][None], k_a=rw_k_a[l][None])
        if l > 0:
            prm.update(v0=rw_v0[l - 1][None], v1=_bf(_pad_cols(rw_v1[l - 1], LANES)),
                       v2=_bf(_pad_rows(rw_v2[l - 1], LANES)))
        rw_norm = (rw_ln_w[l][None], rw_ln_b[l][None], rw_r_k[l].reshape(1, BR_W))
        ys = [None, None, None]
        for gi, grp in enumerate(groups):
            st_rkv, st_lora = _pack_shift_cols(state_shift[l]) if grp.has_state else (None, None)
            ys[0], st_out[gi][0] = _hgrn(z, grp, l, hg_lower_bounds, g_hg, state_hgrn, (ys[0], st_out[gi][0]))
            *seqs, last_rkv, last_lora = _rw_prep(z, grp, prm, st_rkv, st_lora, v_first[gi])
            shifts[gi].append(_unpack_shift_cols(last_rkv[:, 0], last_lora[:, 0]))
            if l == 0:
                v_first[gi] = seqs[3]
            if grp.has_state:
                ys[1], st_out[gi][1] = _rw_scan(seqs, grp, rw_norm, state_rwkv_t, l, (ys[1], st_out[gi][1]))
            else:
                ys[1], st_out[gi][1] = _rw_scan_chunked(seqs, grp, rw_norm, l, (ys[1], st_out[gi][1]))
            ys[2], st_out[gi][2] = _retention(z, grp, tables[gi], state_ret, l, (ys[2], st_out[gi][2]))
        m = _merge(ys, z, wb, l)
        x = _post(x, m, p, wo, g_post, pu, g_ple, pg, l)

    y_prompt = x[:N_PROMPT].reshape(BATCH, SEQ, D_MODEL)
    y_sample = x[N_PROMPT:].reshape(DEC_BATCH, DEC_SEQ, D_MODEL)
    (hg_p, rw_p, rn_p), (hg_s, rw_s, rn_s) = st_out
    rw_p = jnp.swapaxes(rw_p, -1, -2)
    rw_s = jnp.transpose(rw_s, (0, 4, 1, 2, 3))
    sh_p, sh_s = (jnp.stack(s, axis=0) for s in shifts)
    return (y_prompt, y_sample, hg_p, hg_s, rw_p, rw_s, sh_p, sh_s, rn_p, rn_s)
```

```python
import functools

import numpy as np
import jax
import jax.numpy as jnp
from jax import lax
from jax.experimental import pallas as pl
from jax.experimental.pallas import tpu as pltpu

F32 = jnp.float32
BF16 = jnp.bfloat16
HIGHEST = lax.Precision.HIGHEST

D_MODEL = 2048
BATCH, SEQ = 4, 2048
DEC_BATCH, DEC_SEQ = 128, 8
DEPTH = 2
PAST_LEN = 16384
N_BRANCH = 3
BR_W = 1024
HG_HEADS, HG_DK, HG_DV = 8, 128, 128
F_MIN = 1e-30
RW_HEADS, RW_N = 16, 64
RW_LORA = 96
RW_MV_LORA = 64
RW_GN_EPS = 64e-5
RN_HEADS, RN_DK, RN_DV = 4, 256, 256
ROPE_BASE = 10000.0
CHUNK = 64
PLE_DIM = 256
EPS = 1e-6

LANES = 128
SUBLANES = 8
MXU_DIM = 256

LORA_PAD = LANES
Z_HG = 0
Z_RKV = Z_HG + 3 * BR_W
Z_RN = Z_RKV + 3 * BR_W
Z_GATE = Z_RN + 3 * BR_W
Z_MERGE = Z_GATE + N_BRANCH * BR_W
Z_LORA = Z_MERGE + N_BRANCH * D_MODEL
Z_USED = Z_LORA + 2 * LORA_PAD
INPROJ_TN = 512
Z_W = -(-Z_USED // INPROJ_TN) * INPROJ_TN
INPROJ_TM = 1024

N_PROMPT = BATCH * SEQ
N_SAMPLE = DEC_BATCH * DEC_SEQ
N_ROWS = N_PROMPT + N_SAMPLE


def _bf(x):
    return x.astype(BF16)


def _dot(a, b):
    return jnp.dot(_bf(a), _bf(b), preferred_element_type=F32)


def _dot_nt(a, b):
    return lax.dot_general(_bf(a), _bf(b), (((1,), (1,)), ((), ())), preferred_element_type=F32)


def _dot_tn(a, b):
    return lax.dot_general(_bf(a), _bf(b), (((0,), (0,)), ((), ())), preferred_element_type=F32)


def _split_dot(x, g):
    hi = _bf(x)
    lo = _bf(x - hi.astype(F32))
    return (jnp.dot(hi, g, preferred_element_type=F32) + jnp.dot(lo, g, preferred_element_type=F32))


def _bf_terms(x):
    hi = _bf(x)
    r1 = x - hi.astype(F32)
    mid = _bf(r1)
    return hi, mid, _bf(r1 - mid.astype(F32))


def _sum_dot(m, x):
    m = _bf(m)
    return sum(jnp.dot(m, t, preferred_element_type=F32) for t in _bf_terms(x))


def _sum_dot_tn(x, m):
    m = _bf(m)
    return sum(lax.dot_general(t, m, (((0,), (0,)), ((), ())), preferred_element_type=F32) for t in _bf_terms(x))


def _head_ones(width, head):
    r = lax.broadcasted_iota(jnp.int32, (width, width), 0) // head
    c = lax.broadcasted_iota(jnp.int32, (width, width), 1) // head
    return (r == c).astype(BF16)


def _rms(x):
    return x * lax.rsqrt(jnp.mean(x * x, axis=-1, keepdims=True) + EPS)


def _round_robin(gens):
    live = list(gens)
    while live:
        nxt = []
        for gen in live:
            try:
                next(gen)
                nxt.append(gen)
            except StopIteration:
                pass
        live = nxt


class _Group:
    def __init__(self, name, row0, batch, seq, pos0, has_state):
        self.name, self.row0, self.batch, self.seq, self.pos0, self.has_state = name, row0, batch, seq, pos0, has_state
        self.rows = batch * seq


PROMPT = _Group("prompt", 0, BATCH, SEQ, 0, False)
SAMPLE = _Group("sample", N_PROMPT, DEC_BATCH, DEC_SEQ, PAST_LEN, True)


def _pcall(body, *, name, grid, in_specs, args, out_specs, out_shape, into=None, scratch=(), semantics,
           vmem_mb=None):
    in_specs, args = list(in_specs), list(args)
    n_in, aliases = len(args), {}
    for k, arr in enumerate(into or ()):
        if arr is not None:
            aliases[len(args)] = k
            in_specs.append(pl.BlockSpec(memory_space=pl.ANY))
            args.append(arr)
    n_alias = len(aliases)

    def kernel_fn(*refs):
        return body(*refs[:n_in], *refs[n_in + n_alias:])

    params = dict(dimension_semantics=semantics)
    if vmem_mb is not None:
        params['vmem_limit_bytes'] = vmem_mb << 20
    return pl.pallas_call(
        kernel_fn, grid=grid, in_specs=in_specs, out_specs=out_specs, out_shape=out_shape,
        scratch_shapes=list(scratch), input_output_aliases=aliases,
        compiler_params=pltpu.CompilerParams(**params), name=name)(*args)


def _layer_block(layer, shape, index_map):
    return pl.BlockSpec((None,) + tuple(shape), lambda *g: (layer,) + tuple(index_map(*g)))


D_IN = 3 * BR_W + (3 * BR_W + 2 * RW_LORA) + 3 * BR_W + N_BRANCH * BR_W + N_BRANCH * D_MODEL
PACK_ROWS = 128
PACK_COLS = 2048


def _pack_kernel(w_ref, o_ref):
    o_rw = 3 * BR_W

    def copy(dst, src, n):
        for c in range(0, n, PACK_COLS):
            m = min(PACK_COLS, n - c)
            o_ref[:, dst + c:dst + c + m] = _bf(w_ref[src + c:src + c + m, :].T)

    def lora(src):
        t = w_ref[src:src + LORA_PAD, :].T
        keep = lax.broadcasted_iota(jnp.int32, t.shape, 1) < RW_LORA
        return _bf(jnp.where(keep, t, 0.0))

    copy(Z_HG, 0, o_rw + BR_W)
    copy(Z_RKV + BR_W, o_rw + BR_W + RW_LORA, 2 * BR_W)
    rest = o_rw + 3 * BR_W + 2 * RW_LORA
    copy(Z_RN, rest, D_IN - rest)
    o_ref[:, Z_LORA:Z_LORA + LORA_PAD] = lora(o_rw + BR_W)
    o_ref[:, Z_LORA + LORA_PAD:Z_USED] = lora(rest - RW_LORA)
    o_ref[:, Z_USED:] = jnp.zeros((PACK_ROWS, Z_W - Z_USED), BF16)


def _pack_w_in(w_in):
    return _pcall(
        _pack_kernel, name="pack_w_in", grid=(DEPTH, D_MODEL // PACK_ROWS),
        in_specs=[pl.BlockSpec((None, D_IN, PACK_ROWS), lambda l, i: (l, 0, i))], args=[jnp.swapaxes(w_in, 1, 2)],
        out_specs=pl.BlockSpec((None, PACK_ROWS, Z_W), lambda l, i: (l, i, 0)),
        out_shape=jax.ShapeDtypeStruct((DEPTH, D_MODEL, Z_W), BF16),
        semantics=("parallel", "parallel"), vmem_mb=48)


def _pack_shift_cols(a):
    r = a[..., :BR_W]
    wlo = a[..., BR_W:BR_W + RW_LORA]
    k = a[..., BR_W + RW_LORA:2 * BR_W + RW_LORA]
    v = a[..., 2 * BR_W + RW_LORA:3 * BR_W + RW_LORA]
    alo = a[..., 3 * BR_W + RW_LORA:]
    zl = jnp.zeros(a.shape[:-1] + (LORA_PAD - RW_LORA,), a.dtype)
    return jnp.concatenate([r, k, v], axis=-1), jnp.concatenate([wlo, zl, alo, zl], axis=-1)


def _unpack_shift_cols(rkv, lora):
    return jnp.concatenate([rkv[:, :BR_W], lora[:, :RW_LORA], rkv[:, BR_W:],
                            lora[:, LORA_PAD:LORA_PAD + RW_LORA]], axis=1)


def _row_pair_maps(tile):
    npt = N_PROMPT // tile
    return (lambda i, *_: (jnp.minimum(i, npt - 1), 0)), (lambda i, *_: (jnp.maximum(i - npt, 0), 0)), npt


def _inproj_kernel(xp_ref, xs_ref, g_ref, w_ref, o_ref, h_ref, *, npt):
    @pl.when(pl.program_id(1) == 0)
    def _():
        x = jnp.where(pl.program_id(0) < npt, xp_ref[...], xs_ref[...])
        h_ref[...] = _bf(_rms(x) * g_ref[...])

    o_ref[...] = jnp.dot(h_ref[...], w_ref[...], preferred_element_type=F32)


def _inproj(x_pair, g, w, layer):
    pmap, smap, npt = _row_pair_maps(INPROJ_TM)
    return _pcall(
        functools.partial(_inproj_kernel, npt=npt), name="inproj", grid=(N_ROWS // INPROJ_TM, Z_W // INPROJ_TN),
        in_specs=[pl.BlockSpec((INPROJ_TM, D_MODEL), pmap),
                  pl.BlockSpec((INPROJ_TM, D_MODEL), smap),
                  _layer_block(layer, (1, D_MODEL), lambda i, j: (0, 0)),
                  _layer_block(layer, (D_MODEL, INPROJ_TN), lambda i, j: (0, j))],
        args=[*x_pair, g, w],
        out_specs=pl.BlockSpec((INPROJ_TM, INPROJ_TN), lambda i, j: (i, j)),
        out_shape=jax.ShapeDtypeStruct((N_ROWS, Z_W), F32),
        scratch=[pltpu.VMEM((INPROJ_TM, D_MODEL), BF16)],
        semantics=("parallel", "arbitrary"), vmem_mb=56)


def _gla_level_matrix(c):
    nlv = c.bit_length() - 1
    m = np.zeros(((nlv + 1) * c, c), np.float32)
    for r in range(c):
        m[r, :r + 1] = 1.0
        for p in range(nlv):
            bd = ((r >> (p + 1)) << (p + 1)) + (1 << p) - 1
            if (r >> p) & 1:
                m[(p + 1) * c + r, bd + 1:r + 1] = 1.0
            else:
                m[(p + 1) * c + r, r + 1:bd + 1] = 1.0
    return m


def _hgrn_kernel(*refs, c, nb, layer, has_state):
    if has_state:
        hq_ref, hf_ref, hi_ref, lbp_ref, gn_ref, ms_ref, s0_ref, y_ref, so_ref, s_scr = refs
    else:
        hq_ref, hf_ref, hi_ref, lbp_ref, gn_ref, ms_ref, y_ref, so_ref, s_scr = refs
    nlv = c.bit_length() - 1
    ci = pl.program_id(1)

    @pl.when(ci == 0)
    def _():
        if has_state:
            s_scr[...] = s0_ref[...]
        else:
            s_scr[...] = jnp.zeros_like(s_scr)

    lbp = lbp_ref[...]
    e = jnp.exp(lbp - jnp.max(lbp, axis=0, keepdims=True))
    sm = e / jnp.sum(e, axis=0, keepdims=True)
    lb = jnp.sum(sm[0:layer + 1], axis=0, keepdims=True) - sm[0:1]

    ti = lax.broadcasted_iota(jnp.int32, (c, c), 0)
    si = lax.broadcasted_iota(jnp.int32, (c, c), 1)
    txs = ti ^ si
    lower = ti > si
    ones_c = jnp.ones((c, LANES), F32)

    for ib in range(nb):
        rows = pl.ds(ib * c, c)
        hq, hf, v = hq_ref[rows, :], hf_ref[rows, :], hi_ref[rows, :]
        sig = jax.nn.sigmoid(hf)
        f_gate = lb + (1.0 - lb) * sig
        logf = jnp.log(jnp.maximum(f_gate, F_MIN))
        k = (1.0 - lb) * (1.0 - sig)
        q = jax.nn.silu(hq) * HG_DK ** -0.5
        lv = _sum_dot(ms_ref[...], logf)
        bl_col = _sum_dot_tn(logf, ones_c)
        qk = q * k

        def head(h):
            hs = slice(h * HG_DK, (h + 1) * HG_DK)
            qh, kh, vh, b = q[:, hs], k[:, hs], v[:, hs], lv[0:c, hs]
            att = jnp.where(ti == si, jnp.sum(qk[:, hs], axis=-1, keepdims=True), 0.0)
            for p in range(nlv):
                ex = jnp.exp(lv[(p + 1) * c:(p + 2) * c, hs])
                att = att + jnp.where(((txs >> p) == 1) & lower, _dot_nt(qh * ex, kh * ex), 0.0)
            yield
            s = s_scr[ib, h]
            o = _dot(att, vh) + _dot(qh * jnp.exp(b), s)
            bl = b[c - 1:c]
            s_scr[ib, h] = jnp.exp(bl_col[hs, :]) * s + _dot_tn(kh * jnp.exp(bl - b), vh)
            yield
            y_ref[rows, hs] = _rms(o) * gn_ref[...]

        _round_robin([head(h) for h in range(HG_HEADS)])

    @pl.when(ci == pl.num_programs(1) - 1)
    def _():
        so_ref[...] = s_scr[...]


def _hgrn(z, grp, layer, lower_bounds, gnorm, state, into):
    c = min(CHUNK, grp.seq)
    nb = 1 if grp.seq > c else 2
    nchunk = grp.seq // c
    blk = nb * c
    base = grp.row0 // blk

    def zspec(col0):
        return pl.BlockSpec((blk, BR_W), lambda i, ci: (base + i * nchunk + ci, col0 // BR_W))

    in_specs = [zspec(Z_HG), zspec(Z_HG + BR_W), zspec(Z_HG + 2 * BR_W),
                pl.BlockSpec((DEPTH, BR_W), lambda i, ci: (0, 0)),
                _layer_block(layer, (1, HG_DV), lambda i, ci: (0, 0)),
                pl.BlockSpec(((c.bit_length()) * c, c), lambda i, ci: (0, 0))]
    args = [z, z, z, lower_bounds, gnorm, jnp.asarray(_gla_level_matrix(c))]
    st_spec = _layer_block(layer, (nb, HG_HEADS, HG_DK, HG_DV), lambda i, ci: (i, 0, 0, 0))
    if grp.has_state:
        in_specs.append(st_spec)
        args.append(state)
    return _pcall(
        functools.partial(_hgrn_kernel, c=c, nb=nb, layer=layer, has_state=grp.has_state),
        name="hgrn_" + grp.name, grid=(grp.batch // nb, nchunk), in_specs=in_specs, args=args,
        out_specs=[pl.BlockSpec((blk, BR_W), lambda i, ci: (base + i * nchunk + ci, 0)), st_spec],
        out_shape=[jax.ShapeDtypeStruct((N_ROWS, BR_W), F32),
                   jax.ShapeDtypeStruct((DEPTH, grp.batch, HG_HEADS, HG_DK, HG_DV), F32)],
        into=into, scratch=[pltpu.VMEM((nb, HG_HEADS, HG_DK, HG_DV), F32)],
        semantics=("parallel", "arbitrary"), vmem_mb=48)


def _rope_kernel(inv_ref, cos_ref, sin_ref, *, pos0):
    t = cos_ref.shape[0]
    pos = (pos0 + lax.broadcasted_iota(jnp.int32, (t, RN_DK // 2), 0)).astype(F32)
    ang = pos * inv_ref[...]
    cos_ref[...] = jnp.cos(ang)
    sin_ref[...] = jnp.sin(ang)


def _rope_tables(grp):
    inv = ROPE_BASE ** (-jnp.arange(0, RN_DK, 2, dtype=F32) / RN_DK)
    shp = jax.ShapeDtypeStruct((grp.seq, RN_DK // 2), F32)
    return pl.pallas_call(functools.partial(_rope_kernel, pos0=grp.pos0), out_shape=[shp, shp],
                          name="rope")(inv[None, :])


def _ret_kernel(*refs, c, nb, has_state):
    if has_state:
        (q_ref, k_ref, v_ref, cos_ref, sin_ref, dm_ref, qd_ref, kd_ref, cd_ref, s0_ref,
         y_ref, so_ref, s_scr) = refs
    else:
        (q_ref, k_ref, v_ref, cos_ref, sin_ref, dm_ref, qd_ref, kd_ref, cd_ref,
         y_ref, so_ref, s_scr) = refs
    ci = pl.program_id(1)

    @pl.when(ci == 0)
    def _():
        if has_state:
            s_scr[...] = s0_ref[...]
        else:
            s_scr[...] = jnp.zeros_like(s_scr)

    cos, sin = cos_ref[...], sin_ref[...]
    half = RN_DK // 2

    def rot(x):
        x1, x2 = x[:, :half], x[:, half:]
        return jnp.concatenate([x1 * cos - x2 * sin, x1 * sin + x2 * cos], axis=1)

    def head(ib, h):
        rows, hs = pl.ds(ib * c, c), slice(h * RN_DK, (h + 1) * RN_DK)
        q = rot(q_ref[rows, hs])
        k = rot(k_ref[rows, hs]) * RN_DK ** -0.5
        v = v_ref[rows, hs]
        s = s_scr[ib, h]
        att = _dot_nt(q, k) * dm_ref[h]
        qs = _dot(q, s)
        s_scr[ib, h] = cd_ref[h] * s + _dot_tn(k * kd_ref[h], v)
        yield
        o = _dot(att, v) + qs * qd_ref[h]
        yield
        y_ref[rows, hs] = _rms(o)

    _round_robin([head(ib, h) for ib in range(nb) for h in range(RN_HEADS)])

    @pl.when(ci == pl.num_programs(1) - 1)
    def _():
        so_ref[...] = s_scr[...]


def _retention(z, grp, tables, state, layer, into):
    c = min(CHUNK, grp.seq)
    nb = 1 if grp.seq > c else 2
    nchunk = grp.seq // c
    blk = nb * c
    base = grp.row0 // blk
    log_gamma = jnp.log(1.0 - 2.0 ** (-5.0 - jnp.arange(RN_HEADS, dtype=F32)))
    j = jnp.arange(c, dtype=F32)
    rel = j[:, None] - j[None, :]
    dmat = jnp.where(rel >= 0, jnp.exp(log_gamma[:, None, None] * jnp.maximum(rel, 0.0)), 0.0)
    q_dec = jnp.exp(log_gamma[:, None] * (j + 1.0))[..., None]
    k_dec = jnp.exp(log_gamma[:, None] * (c - 1.0 - j))[..., None]
    c_dec = jnp.exp(log_gamma * c)[:, None, None]

    def zspec(col0):
        return pl.BlockSpec((blk, BR_W), lambda i, ci: (base + i * nchunk + ci, col0 // BR_W))

    def const(shape):
        return pl.BlockSpec(shape, lambda i, ci: (0,) * len(shape))

    tspec = pl.BlockSpec((c, RN_DK // 2), lambda i, ci: (ci, 0))
    in_specs = [zspec(Z_RN), zspec(Z_RN + BR_W), zspec(Z_RN + 2 * BR_W), tspec, tspec,
                const((RN_HEADS, c, c)), const((RN_HEADS, c, 1)), const((RN_HEADS, c, 1)), const((RN_HEADS, 1, 1))]
    args = [z, z, z, tables[0], tables[1], dmat, q_dec, k_dec, c_dec]
    st_spec = _layer_block(layer, (nb, RN_HEADS, RN_DK, RN_DV), lambda i, ci: (i, 0, 0, 0))
    if grp.has_state:
        in_specs.append(st_spec)
        args.append(state)
    return _pcall(
        functools.partial(_ret_kernel, c=c, nb=nb, has_state=grp.has_state),
        name="ret_" + grp.name, grid=(grp.batch // nb, nchunk), in_specs=in_specs, args=args,
        out_specs=[pl.BlockSpec((blk, BR_W), lambda i, ci: (base + i * nchunk + ci, 0)), st_spec],
        out_shape=[jax.ShapeDtypeStruct((N_ROWS, BR_W), F32),
                   jax.ShapeDtypeStruct((DEPTH, grp.batch, RN_HEADS, RN_DK, RN_DV), F32)],
        into=into, scratch=[pltpu.VMEM((nb, RN_HEADS, RN_DK, RN_DV), F32)],
        semantics=("parallel", "arbitrary"), vmem_mb=48)


RW_PREP_ROWS = 256


def _rw_token_mix(ur, ul, w0_ref, w2_ref, a0_ref, a2_ref, kk_ref, ka_ref, vmix):
    r, k, v = ur[:, :BR_W], ur[:, BR_W:2 * BR_W], ur[:, 2 * BR_W:]
    wlo, alo = ul[:, :LORA_PAD], ul[:, LORA_PAD:]
    w = -jax.nn.softplus(-(w0_ref[...] + _dot(jnp.tanh(wlo), w2_ref[...]))) - 0.5
    a = jax.nn.sigmoid(a0_ref[...] + _dot(alo, a2_ref[...]))
    kk = k * kk_ref[...]
    ones = _head_ones(MXU_DIM, RW_N)
    kk2 = kk * kk
    nrm2 = jnp.concatenate(
        [_split_dot(kk2[:, g * MXU_DIM:(g + 1) * MXU_DIM], ones) for g in range(BR_W // MXU_DIM)], axis=1)
    kk = kk / jnp.maximum(jnp.sqrt(nrm2), 1e-12)
    k = k * (1.0 + (a - 1.0) * ka_ref[...])
    if vmix is not None:
        v0_ref, v1_ref, v2_ref, vf_ref = vmix
        mix = jax.nn.sigmoid(v0_ref[...] + _dot(_dot(v, v1_ref[...]), v2_ref[...]))
        v = v + (vf_ref[...] - v) * mix
    return r, -jnp.exp(w), k, v, -kk, kk * a


def _rw_prep_kernel(*refs, tt, nb, has_state, with_vmix):
    refs = list(refs)
    zr_ref, zl_ref = refs[:2]
    pos = 2
    if has_state:
        sr_ref, sl_ref = refs[pos:pos + 2]
        pos += 2
    (mur_ref, mul_ref, w0_ref, w2_ref, a0_ref, a2_ref, kk_ref, ka_ref) = refs[pos:pos + 8]
    pos += 8
    if with_vmix:
        v0_ref, v1_ref, v2_ref, vf_ref = refs[pos:pos + 4]
        pos += 4
    r_ref, d_ref, k_ref, v_ref, a_ref, b_ref, lastr_ref, lastl_ref = refs[pos:pos + 8]
    pr_scr, pl_scr = refs[pos + 8:pos + 10]
    ti = pl.program_id(1)
    rows = nb * tt

    zr, zl = zr_ref[...], zl_ref[...]
    for ib in range(nb):
        lastr_ref[ib] = zr_ref[pl.ds(ib * tt + tt - 1, 1), :]
        lastl_ref[ib] = zl_ref[pl.ds(ib * tt + tt - 1, 1), :]
    first = (lax.broadcasted_iota(jnp.int32, (rows, 1), 0) % tt) == 0

    def prev_rows(z, st_ref, scr):
        w = z.shape[1]
        if nb > 1:
            st = st_ref[...]
            return jnp.broadcast_to(st[:, None, :], (nb, tt, w)).reshape(rows, w)

        @pl.when(ti == 0)
        def _():
            scr[0:1, :] = st_ref[...] if has_state else jnp.zeros((1, w), F32)

        return scr[0:1, :]

    def shifted(z, st_ref, scr, mu):
        prev = jnp.where(first, prev_rows(z, st_ref, scr), pltpu.roll(z, 1, 0))
        scr[0:1, :] = z[rows - 1:rows, :]
        return z + (prev - z) * mu

    ur = shifted(zr, sr_ref if has_state else None, pr_scr, mur_ref[...])
    ul = shifted(zl, sl_ref if has_state else None, pl_scr, mul_ref[...])
    vmix = (v0_ref, v1_ref, v2_ref, vf_ref) if with_vmix else None
    outs = _rw_token_mix(ur, ul, w0_ref, w2_ref, a0_ref, a2_ref, kk_ref, ka_ref, vmix)
    for ref, val in zip((r_ref, d_ref, k_ref, v_ref, a_ref, b_ref), outs):
        ref[...] = val


def _rw_prep(z, grp, prm, state_rkv, state_lora, v_first):
    tt = min(RW_PREP_ROWS, grp.seq)
    nb = RW_PREP_ROWS // tt
    ntile = grp.seq // tt
    rows = nb * tt
    base = grp.row0 // rows
    with_vmix = v_first is not None
    w_rkv, w_lora = 3 * BR_W, 2 * LORA_PAD

    def rowmap(i, t):
        return base + i * ntile + t

    def const(shape):
        return pl.BlockSpec(shape, lambda i, t: (0,) * len(shape))

    in_specs = [pl.BlockSpec((rows, w_rkv), lambda i, t: (rowmap(i, t), Z_RKV // w_rkv)),
                pl.BlockSpec((rows, w_lora), lambda i, t: (rowmap(i, t), Z_LORA // w_lora))]
    args = [z, z]
    if grp.has_state:
        in_specs += [pl.BlockSpec((nb, w_rkv), lambda i, t: (i, 0)),
                     pl.BlockSpec((nb, w_lora), lambda i, t: (i, 0))]
        args += [state_rkv, state_lora]
    in_specs += [const((1, w_rkv)), const((1, w_lora)), const((1, BR_W)), const((LORA_PAD, BR_W)),
                 const((1, BR_W)), const((LORA_PAD, BR_W)), const((1, BR_W)), const((1, BR_W))]
    args += [prm['mu_rkv'], prm['mu_lora'], prm['w0'], prm['w2'], prm['a0'], prm['a2'], prm['k_k'], prm['k_a']]
    out_spec = pl.BlockSpec((rows, BR_W), lambda i, t: (i * ntile + t, 0))
    if with_vmix:
        in_specs += [const((1, BR_W)), const((BR_W, LANES)), const((LANES, BR_W)), out_spec]
        args += [prm['v0'], prm['v1'], prm['v2'], v_first]
    out = jax.ShapeDtypeStruct((grp.rows, BR_W), F32)

    def last_spec(w):
        return pl.BlockSpec((nb, 1, w), lambda i, t: (i, 0, 0))

    def last_shape(w):
        return jax.ShapeDtypeStruct((grp.batch, 1, w), F32)

    return pl.pallas_call(
        functools.partial(_rw_prep_kernel, tt=tt, nb=nb, has_state=grp.has_state, with_vmix=with_vmix),
        grid=(grp.batch // nb, ntile),
        in_specs=in_specs,
        out_specs=[out_spec] * 6 + [last_spec(w_rkv), last_spec(w_lora)],
        out_shape=[out] * 6 + [last_shape(w_rkv), last_shape(w_lora)],
        scratch_shapes=[pltpu.VMEM((SUBLANES, w_rkv), F32), pltpu.VMEM((SUBLANES, w_lora), F32)],
        compiler_params=pltpu.CompilerParams(
            dimension_semantics=("parallel", "arbitrary"), vmem_limit_bytes=48 << 20),
        name="rw_prep_" + ("sample" if grp.has_state else "prompt"),
    )(*args)


RW_TILE = MXU_DIM
RW_TILE_HEADS = RW_TILE // RW_N
RW_LANE_HEADS = LANES // RW_N
RW_SCAN_UNROLL = 4


def _rw_scan_kernel(r_ref, d_ref, k_ref, v_ref, a_ref, b_ref, lw_ref, lb_ref, rk_ref, s0_ref,
                    y_ref, so_ref, s_scr, v_scr, y_scr):
    t = pl.program_id(1)

    @pl.when(t == 0)
    def _():
        s_scr[...] = s0_ref[...]

    r, ld, k, a, b = (ref[...].T for ref in (r_ref, d_ref, k_ref, a_ref, b_ref))
    v_scr[...] = v_ref[...].T
    w = jnp.exp(ld)

    for h in range(RW_LANE_HEADS):
        hs = slice(h * RW_N, (h + 1) * RW_N)
        rh, wh, kh, ah, bh = r[hs], w[hs], k[hs], a[hs], b[hs]

        def row_update(i, carry, h=h, rh=rh, wh=wh, kh=kh, ah=ah, bh=bh):
            s = s_scr[h, i]
            sa = jnp.sum(s * ah, axis=0, keepdims=True)
            s = s * wh + sa * bh + v_scr[pl.ds(h * RW_N + i, 1), :] * kh
            s_scr[h, i] = s
            y_scr[pl.ds(h * RW_N + i, 1), :] = jnp.sum(s * rh, axis=0, keepdims=True)
            return carry

        lax.fori_loop(0, RW_N, row_update, 0, unroll=RW_SCAN_UNROLL)

        y, vh = y_scr[hs, :], v_scr[hs, :]
        mu = jnp.mean(y, axis=0, keepdims=True)
        dlt = y - mu
        var = jnp.mean(dlt * dlt, axis=0, keepdims=True)
        yn = dlt * lax.rsqrt(var + RW_GN_EPS) * lw_ref[h] + lb_ref[h]
        bonus = jnp.sum(rh * kh * rk_ref[h], axis=0, keepdims=True)
        y_scr[hs, :] = yn + bonus * vh

    y_ref[...] = y_scr[...].T

    @pl.when(t == pl.num_programs(1) - 1)
    def _():
        so_ref[...] = s_scr[...]


def _rw_scan(seqs, grp, norm, state_t, layer, into):
    assert grp.has_state and grp.batch == LANES
    hp = RW_LANE_HEADS
    width = hp * RW_N
    tm = jnp.stack(seqs).reshape(len(seqs), grp.batch, grp.seq, BR_W).transpose(0, 2, 1, 3)
    seq_specs = [pl.BlockSpec((None, None, grp.batch, width), lambda j, t, n=n: (n, t, 0, j))
                 for n in range(len(seqs))]
    par_spec = pl.BlockSpec((hp, RW_N, 1), lambda j, t: (j, 0, 0))
    st_spec = _layer_block(layer, (hp, RW_N, RW_N, grp.batch), lambda j, t: (j, 0, 0, 0))
    y, st = _pcall(
        _rw_scan_kernel, name="rw_scan_" + grp.name, grid=(RW_HEADS // hp, grp.seq),
        in_specs=seq_specs + [par_spec] * 3 + [st_spec],
        args=[tm] * len(seqs) + [p.reshape(RW_HEADS, RW_N, 1) for p in norm] + [state_t],
        out_specs=[pl.BlockSpec((None, grp.batch, width), lambda j, t: (t, 0, j)), st_spec],
        out_shape=[jax.ShapeDtypeStruct((grp.seq, grp.batch, BR_W), F32),
                   jax.ShapeDtypeStruct((DEPTH, RW_HEADS, RW_N, RW_N, grp.batch), F32)],
        into=(None, into[1]),
        scratch=[pltpu.VMEM((hp, RW_N, RW_N, grp.batch), F32), pltpu.VMEM((width, grp.batch), F32),
                 pltpu.VMEM((width, grp.batch), F32)],
        semantics=("parallel", "arbitrary"), vmem_mb=48)
    y_rows = y.transpose(1, 0, 2).reshape(grp.rows, BR_W)
    return lax.dynamic_update_slice(into[0], y_rows, (grp.row0, 0)), st


RW_CHUNK = RW_N
RW_CHUNKS_PER_STEP = 2
RW_SOLVE_BASE = 8


def _rw_chunk_kernel(*refs, with_vmix):
    refs = list(refs)
    zr_ref, zl_ref, mur_ref, mul_ref, w0_ref, w2_ref, a0_ref, a2_ref, kk_ref, ka_ref = refs[:10]
    pos = 10
    vmix = None
    if with_vmix:
        vmix = tuple(refs[pos:pos + 4])
        pos += 4
    lw_ref, lb_ref, rk_ref, y_ref, ho_ref, vout_ref, lastr_ref, lastl_ref = refs[pos:pos + 8]
    h_scr, seq_scr, pr_scr, pl_scr = refs[pos + 8:pos + 12]
    ci = pl.program_id(1)
    ln, tw, nt = RW_CHUNK, RW_TILE, BR_W // RW_TILE
    nrows = ln * RW_CHUNKS_PER_STEP

    @pl.when(ci == 0)
    def _():
        h_scr[...] = jnp.zeros_like(h_scr)
        pr_scr[...] = jnp.zeros_like(pr_scr)
        pl_scr[...] = jnp.zeros_like(pl_scr)

    first = lax.broadcasted_iota(jnp.int32, (nrows, 1), 0) == 0

    def shifted(z_ref, scr, mu, last_ref):
        z = z_ref[...]
        prev = jnp.where(first, scr[0:1, :], pltpu.roll(z, 1, 0))
        scr[0:1, :] = z[nrows - 1:nrows, :]
        last_ref[0] = z[nrows - 1:nrows, :]
        return z + (prev - z) * mu

    ur = shifted(zr_ref, pr_scr, mur_ref[...], lastr_ref)
    ul = shifted(zl_ref, pl_scr, mul_ref[...], lastl_ref)
    for n, val in enumerate(_rw_token_mix(ur, ul, w0_ref, w2_ref, a0_ref, a2_ref, kk_ref, ka_ref, vmix)):
        seq_scr[n] = val
    vout_ref[...] = seq_scr[3]

    row = lax.broadcasted_iota(jnp.int32, (tw, tw), 0)
    col = lax.broadcasted_iota(jnp.int32, (tw, tw), 1)
    same_head = (row // RW_N) == (col // RW_N)
    ones = same_head.astype(BF16)
    t_i = lax.broadcasted_iota(jnp.int32, (ln, tw), 0)
    s_i = lax.broadcasted_iota(jnp.int32, (ln, tw), 1) % ln
    strict, incl = s_i < t_i, s_i <= t_i
    tril = (lax.broadcasted_iota(jnp.int32, (ln, ln), 0) >= lax.broadcasted_iota(jnp.int32, (ln, ln), 1)).astype(F32)

    def blocks(x):
        return jnp.where(same_head, jnp.concatenate([x] * RW_TILE_HEADS, axis=0), 0.0).astype(BF16)

    def mm(x, w):
        return jnp.dot(_bf(x), w, preferred_element_type=F32)

    def prepare(j, g, out):
        rows, sl = pl.ds(j * ln, ln), slice(g * tw, (g + 1) * tw)
        r, ld, k, v, a, b = (seq_scr[n, rows, sl] for n in range(6))
        c = _sum_dot(tril, ld)
        tot_col = _sum_dot_tn(ld, jnp.ones((ln, LANES), F32))
        yield
        c_last = c[ln - 1:ln]
        e_inv, e_end = jnp.exp(-c), jnp.exp(c_last - c)
        at, rt = a * jnp.exp(c - ld), r * jnp.exp(c)
        bt, kt, bp, kp = b * e_inv, k * e_inv, b * e_end, k * e_end
        gram = lax.dot_general(_bf(jnp.concatenate([at, rt], axis=0)),
                               jnp.concatenate([blocks(bt), blocks(kt)], axis=0),
                               (((1,), (1,)), ((), ())), preferred_element_type=F32)
        yield
        n_ab = jnp.where(strict, gram[:ln, :tw], 0.0)
        a_ak = jnp.where(strict, gram[:ln, tw:], 0.0)
        a_rb = jnp.where(incl, gram[ln:, :tw], 0.0)
        a_rk = jnp.where(incl, gram[ln:, tw:], 0.0)
        av = mm(jnp.concatenate([a_ak, a_rk], axis=0), blocks(v))
        yield
        def same_block(m):
            return (t_i // m) == (s_i // m)

        nd = jnp.where(same_block(RW_SOLVE_BASE), n_ab, 0.0)
        tinv = jnp.where(s_i == t_i, 1.0, 0.0) + nd
        p = mm(nd, blocks(nd))
        yield
        m = 2
        while m < RW_SOLVE_BASE:
            m *= 2
            if m < RW_SOLVE_BASE:
                both = mm(jnp.concatenate([tinv, p], axis=0), blocks(p))
                tinv, p = tinv + both[:ln], both[ln:]
            else:
                tinv = tinv + mm(tinv, blocks(p))
            yield
        m = RW_SOLVE_BASE
        while m < ln:
            lower_left = jnp.where(same_block(2 * m) & jnp.logical_not(same_block(m)), n_ab, 0.0)
            tn = mm(tinv, blocks(lower_left))
            yield
            tinv = tinv + mm(tn, blocks(tinv))
            yield
            m *= 2
        w = mm(tinv, jnp.concatenate([blocks(at), blocks(av[:ln])], axis=1))
        yield
        at2, uv = w[:, :tw], w[:, tw:]
        m_off = jnp.where(same_head, _dot_tn(bp, at2), 0.0)
        cc = jnp.where(same_head, _dot_tn(jnp.concatenate([bp, kp], axis=0), jnp.concatenate([uv, v], axis=0)), 0.0)
        qy = mm(a_rb, jnp.concatenate([blocks(at2), blocks(uv)], axis=1))
        decay_col = jnp.concatenate([jnp.exp(tot_col)] * (tw // LANES), axis=1)
        out[j, g] = (rt + qy[:, :tw], qy[:, tw:] + av[ln:], m_off, cc, decay_col)

    items = [(j, g) for j in range(RW_CHUNKS_PER_STEP) for g in range(nt)]
    prepared = {}
    _round_robin([prepare(j, g, prepared) for j, g in items])

    for j, g in items:
        q, yc, m_off, cc, decay_col = prepared[(j, g)]
        rows, sl = pl.ds(j * ln, ln), slice(g * tw, (g + 1) * tw)
        h = h_scr[g]
        qm = mm(jnp.concatenate([q, m_off], axis=0), _bf(h))
        y = qm[:ln] + yc
        h_scr[g] = decay_col * h + qm[ln:] + cc

        mu = _split_dot(y, ones) * (1.0 / RW_N)
        dlt = y - mu
        var = _split_dot(dlt * dlt, ones) * (1.0 / RW_N)
        yn = dlt * lax.rsqrt(var + RW_GN_EPS) * lw_ref[:, sl] + lb_ref[:, sl]
        bonus = _split_dot(seq_scr[0, rows, sl] * seq_scr[2, rows, sl] * rk_ref[:, sl], ones)
        y_ref[rows, sl] = yn + bonus * seq_scr[3, rows, sl]

    @pl.when(ci == pl.num_programs(1) - 1)
    def _():
        for g in range(nt):
            h = h_scr[g]
            for hd in range(RW_TILE_HEADS):
                ho_ref[0, g * RW_TILE_HEADS + hd] = h[hd * RW_N:(hd + 1) * RW_N, hd * RW_N:(hd + 1) * RW_N]


def _rw_scan_chunked(z, grp, prm, norm, v_first, layer, into):
    blk = RW_CHUNK * RW_CHUNKS_PER_STEP
    assert not grp.has_state and grp.seq % blk == 0 and grp.row0 % blk == 0
    nstep = grp.seq // blk
    base = grp.row0 // blk
    with_vmix = v_first is not None
    w_rkv, w_lora = 3 * BR_W, 2 * LORA_PAD

    def const(shape):
        return pl.BlockSpec(shape, lambda i, c: (0,) * len(shape))

    seq_spec = pl.BlockSpec((blk, BR_W), lambda i, c: (i * nstep + c, 0))
    in_specs = [pl.BlockSpec((blk, w_rkv), lambda i, c: (base + i * nstep + c, Z_RKV // w_rkv)),
                pl.BlockSpec((blk, w_lora), lambda i, c: (base + i * nstep + c, Z_LORA // w_lora)),
                const((1, w_rkv)), const((1, w_lora)), const((1, BR_W)), const((LORA_PAD, BR_W)),
                const((1, BR_W)), const((LORA_PAD, BR_W)), const((1, BR_W)), const((1, BR_W))]
    args = [z, z, prm['mu_rkv'], prm['mu_lora'], prm['w0'], prm['w2'], prm['a0'], prm['a2'], prm['k_k'], prm['k_a']]
    if with_vmix:
        in_specs += [const((1, BR_W)), const((BR_W, LANES)), const((LANES, BR_W)), seq_spec]
        args += [prm['v0'], prm['v1'], prm['v2'], v_first]
    in_specs += [const((1, BR_W))] * 3
    args += list(norm)

    def last_spec(w):
        return pl.BlockSpec((1, 1, w), lambda i, c: (i, 0, 0))

    return _pcall(
        functools.partial(_rw_chunk_kernel, with_vmix=with_vmix),
        name="rw_chunk_" + grp.name, grid=(grp.batch, nstep), in_specs=in_specs, args=args,
        out_specs=[pl.BlockSpec((blk, BR_W), lambda i, c: (base + i * nstep + c, 0)),
                   _layer_block(layer, (1, RW_HEADS, RW_N, RW_N), lambda i, c: (i, 0, 0, 0)),
                   seq_spec, last_spec(w_rkv), last_spec(w_lora)],
        out_shape=[jax.ShapeDtypeStruct((N_ROWS, BR_W), F32),
                   jax.ShapeDtypeStruct((DEPTH, grp.batch, RW_HEADS, RW_N, RW_N), F32),
                   jax.ShapeDtypeStruct((grp.rows, BR_W), F32),
                   jax.ShapeDtypeStruct((grp.batch, 1, w_rkv), F32),
                   jax.ShapeDtypeStruct((grp.batch, 1, w_lora), F32)],
        into=tuple(into) + (None, None, None),
        scratch=[pltpu.VMEM((BR_W // RW_TILE, RW_TILE, RW_TILE), F32), pltpu.VMEM((6, blk, BR_W), F32),
                 pltpu.VMEM((SUBLANES, w_rkv), F32), pltpu.VMEM((SUBLANES, w_lora), F32)],
        semantics=("parallel", "arbitrary"), vmem_mb=48)


MERGE_TM = 256


def _merge_kernel(y0_ref, y1_ref, y2_ref, g_ref, zm_ref, wb_ref, m_ref):
    acc = None
    for n, y_ref in enumerate((y0_ref, y1_ref, y2_ref)):
        ys = y_ref[...] * jax.nn.silu(g_ref[:, n * BR_W:(n + 1) * BR_W])
        proj = jnp.dot(_bf(ys), wb_ref[n], preferred_element_type=F32)
        term = jax.nn.sigmoid(zm_ref[:, n * D_MODEL:(n + 1) * D_MODEL]) * proj
        acc = term if acc is None else acc + term
    m_ref[...] = _bf(acc)


def _merge(ys, z, wb, layer):
    n = z.shape[0]
    yspec = pl.BlockSpec((MERGE_TM, BR_W), lambda i: (i, 0))
    gw, mw = N_BRANCH * BR_W, N_BRANCH * D_MODEL
    return _pcall(
        _merge_kernel, name="merge", grid=(n // MERGE_TM,),
        in_specs=[yspec, yspec, yspec,
                  pl.BlockSpec((MERGE_TM, gw), lambda i: (i, Z_GATE // gw)),
                  pl.BlockSpec((MERGE_TM, mw), lambda i: (i, Z_MERGE // mw)),
                  pl.BlockSpec((None, N_BRANCH, BR_W, D_MODEL), lambda i: (layer, 0, 0, 0),
                               pipeline_mode=pl.Buffered(1))],
        args=[*ys, z, z, wb],
        out_specs=pl.BlockSpec((MERGE_TM, D_MODEL), lambda i: (i, 0)),
        out_shape=jax.ShapeDtypeStruct((n, D_MODEL), BF16),
        semantics=("parallel",), vmem_mb=48)


POST_TM = 256


def _post_kernel(xp_ref, xs_ref, m_ref, pp_ref, ps_ref, wo_ref, np_ref, pu_ref, pn_ref, pg_ref, op_ref, os_ref,
                 *, npt):
    is_prompt = pl.program_id(0) < npt
    x = jnp.where(is_prompt, xp_ref[...], xs_ref[...])
    p = jnp.where(is_prompt, pp_ref[...], ps_ref[...])
    x = x + _rms(jnp.dot(m_ref[...], wo_ref[...], preferred_element_type=F32)) * np_ref[...]
    e = _rms(jnp.dot(_bf(p), pu_ref[...], preferred_element_type=F32)) * pn_ref[...]
    out = x + e * jax.nn.sigmoid(jnp.dot(_bf(x), pg_ref[...], preferred_element_type=F32))

    @pl.when(is_prompt)
    def _():
        op_ref[...] = out

    @pl.when(jnp.logical_not(is_prompt))
    def _():
        os_ref[...] = out


def _post(x_pair, m, p_pair, wo, norm_post, ple_up, ple_norm, ple_gate, layer):
    pmap, smap, npt = _row_pair_maps(POST_TM)

    def const(shape):
        return pl.BlockSpec((None,) + shape, lambda i: (layer, 0, 0), pipeline_mode=pl.Buffered(1))

    def pair(w):
        return [pl.BlockSpec((POST_TM, w), pmap), pl.BlockSpec((POST_TM, w), smap)]

    ple_pair = [_layer_block(layer, (POST_TM, PLE_DIM), pmap), _layer_block(layer, (POST_TM, PLE_DIM), smap)]
    return _pcall(
        functools.partial(_post_kernel, npt=npt), name="post", grid=(N_ROWS // POST_TM,),
        in_specs=pair(D_MODEL) + [pl.BlockSpec((POST_TM, D_MODEL), lambda i: (i, 0))] + ple_pair
        + [const((D_MODEL, D_MODEL)), const((1, D_MODEL)), const((PLE_DIM, D_MODEL)),
           const((1, D_MODEL)), const((D_MODEL, D_MODEL))],
        args=[*x_pair, m, *p_pair, wo, norm_post, ple_up, ple_norm, ple_gate],
        out_specs=pair(D_MODEL),
        out_shape=[jax.ShapeDtypeStruct((N_PROMPT, D_MODEL), F32), jax.ShapeDtypeStruct((N_SAMPLE, D_MODEL), F32)],
        semantics=("arbitrary",), vmem_mb=48)


def _pad_rows(a, rows):
    return jnp.concatenate([a, jnp.zeros((rows - a.shape[0],) + a.shape[1:], a.dtype)], axis=0)


def _pad_cols(a, cols):
    return jnp.concatenate([a, jnp.zeros(a.shape[:-1] + (cols - a.shape[-1],), a.dtype)], axis=-1)


def _layer_rows(a):
    return a.reshape(a.shape[0], 1, a.shape[1])


def kernel(x_prompt, x_sample, state_hgrn, state_rwkv, state_shift, state_ret, p_prompt, p_sample,
           norm_pre, w_in, hg_lower_bounds, hg_norm, rw_mu, rw_w0, rw_w2, rw_a0, rw_a2, rw_k_k, rw_k_a,
           rw_v0, rw_v1, rw_v2, rw_r_k, rw_ln_w, rw_ln_b, w_branch, w_out, norm_post, ple_up, ple_norm, ple_gate):
    groups = (PROMPT, SAMPLE)
    x = (x_prompt.reshape(N_PROMPT, D_MODEL), x_sample.reshape(N_SAMPLE, D_MODEL))
    p = (p_prompt.reshape(DEPTH, N_PROMPT, PLE_DIM), p_sample.reshape(DEPTH, N_SAMPLE, PLE_DIM))
    w_packed = _pack_w_in(w_in)
    wb, wo, pu, pg = _bf(w_branch), _bf(w_out), _bf(ple_up), _bf(ple_gate)
    g_pre, g_hg, g_post, g_ple = (_layer_rows(a) for a in (norm_pre, hg_norm, norm_post, ple_norm))
    tables = [_rope_tables(g) for g in groups]
    state_rwkv_t = jnp.transpose(state_rwkv, (0, 2, 3, 4, 1))
    v_first = [None, None]
    shifts = [[], []]
    st_out = [[None, None, None] for _ in groups]

    for l in range(DEPTH):
        z = _inproj(x, g_pre, w_packed, l)
        mu_rkv, mu_lora = _pack_shift_cols(rw_mu[l][None])
        prm = dict(mu_rkv=mu_rkv, mu_lora=mu_lora, w0=rw_w0[l][None],
                   w2=_bf(_pad_rows(rw_w2[l], LORA_PAD)), a0=rw_a0[l][None],
                   a2=_bf(_pad_rows(rw_a2[l], LORA_PAD)), k_k=rw_k_k[l][None], k_a=rw_k_a[l][None])
        if l > 0:
            prm.update(v0=rw_v0[l - 1][None], v1=_bf(_pad_cols(rw_v1[l - 1], LANES)),
                       v2=_bf(_pad_rows(rw_v2[l - 1], LANES)))
        rw_norm = (rw_ln_w[l][None], rw_ln_b[l][None], rw_r_k[l].reshape(1, BR_W))
        ys = [None, None, None]
        for gi, grp in enumerate(groups):
            st_rkv, st_lora = _pack_shift_cols(state_shift[l]) if grp.has_state else (None, None)
            ys[0], st_out[gi][0] = _hgrn(z, grp, l, hg_lower_bounds, g_hg, state_hgrn, (ys[0], st_out[gi][0]))
            if grp.has_state:
                *seqs, last_rkv, last_lora = _rw_prep(z, grp, prm, st_rkv, st_lora, v_first[gi])
                ys[1], st_out[gi][1] = _rw_scan(seqs, grp, rw_norm, state_rwkv_t, l, (ys[1], st_out[gi][1]))
                v_rows = seqs[3]
            else:
                ys[1], st_out[gi][1], v_rows, last_rkv, last_lora = _rw_scan_chunked(
                    z, grp, prm, rw_norm, v_first[gi], l, (ys[1], st_out[gi][1]))
            shifts[gi].append(_unpack_shift_cols(last_rkv[:, 0], last_lora[:, 0]))
            if l == 0:
                v_first[gi] = v_rows
            ys[2], st_out[gi][2] = _retention(z, grp, tables[gi], state_ret, l, (ys[2], st_out[gi][2]))
        m = _merge(ys, z, wb, l)
        x = _post(x, m, p, wo, g_post, pu, g_ple, pg, l)

    y_prompt = x[0].reshape(BATCH, SEQ, D_MODEL)
    y_sample = x[1].reshape(DEC_BATCH, DEC_SEQ, D_MODEL)
    (hg_p, rw_p, rn_p), (hg_s, rw_s, rn_s) = st_out
    rw_p = jnp.swapaxes(rw_p, -1, -2)
    rw_s = jnp.transpose(rw_s, (0, 4, 1, 2, 3))
    sh_p, sh_s = (jnp.stack(s, axis=0) for s in shifts)
    return (y_prompt, y_sample, hg_p, hg_s, rw_p, rw_s, sh_p, sh_s, rn_p, rn_s)
```

```python
import functools

import numpy as np
import jax
import jax.numpy as jnp
from jax import lax
from jax.experimental import pallas as pl
from jax.experimental.pallas import tpu as pltpu

F32 = jnp.float32
BF16 = jnp.bfloat16
HIGHEST = lax.Precision.HIGHEST

D_MODEL = 2048
BATCH, SEQ = 4, 2048
DEC_BATCH, DEC_SEQ = 128, 8
DEPTH = 2
PAST_LEN = 16384
N_BRANCH = 3
BR_W = 1024
HG_HEADS, HG_DK, HG_DV = 8, 128, 128
F_MIN = 1e-30
RW_HEADS, RW_N = 16, 64
RW_LORA = 96
RW_MV_LORA = 64
RW_GN_EPS = 64e-5
RN_HEADS, RN_DK, RN_DV = 4, 256, 256
ROPE_BASE = 10000.0
CHUNK = 64
PLE_DIM = 256
EPS = 1e-6

LANES = 128
SUBLANES = 8
MXU_DIM = 256

LORA_PAD = LANES
Z_HG = 0
Z_RKV = Z_HG + 3 * BR_W
Z_RN = Z_RKV + 3 * BR_W
Z_GATE = Z_RN + 3 * BR_W
Z_MERGE = Z_GATE + N_BRANCH * BR_W
Z_LORA = Z_MERGE + N_BRANCH * D_MODEL
Z_USED = Z_LORA + 2 * LORA_PAD
INPROJ_TN = 1024
Z_W = -(-Z_USED // INPROJ_TN) * INPROJ_TN
INPROJ_TM = 1024

N_PROMPT = BATCH * SEQ
N_SAMPLE = DEC_BATCH * DEC_SEQ
N_ROWS = N_PROMPT + N_SAMPLE


def _bf(x):
    return x.astype(BF16)


def _dot(a, b):
    return jnp.dot(_bf(a), _bf(b), preferred_element_type=F32)


def _dot_nt(a, b):
    return lax.dot_general(_bf(a), _bf(b), (((1,), (1,)), ((), ())), preferred_element_type=F32)


def _dot_tn(a, b):
    return lax.dot_general(_bf(a), _bf(b), (((0,), (0,)), ((), ())), preferred_element_type=F32)


def _split_dot(x, g):
    hi = _bf(x)
    lo = _bf(x - hi.astype(F32))
    return (jnp.dot(hi, g, preferred_element_type=F32) + jnp.dot(lo, g, preferred_element_type=F32))


def _bf_terms(x):
    hi = _bf(x)
    r1 = x - hi.astype(F32)
    mid = _bf(r1)
    return hi, mid, _bf(r1 - mid.astype(F32))


def _sum_dot(m, x):
    m = _bf(m)
    return sum(jnp.dot(m, t, preferred_element_type=F32) for t in _bf_terms(x))


def _sum_dot_tn(x, m):
    m = _bf(m)
    return sum(lax.dot_general(t, m, (((0,), (0,)), ((), ())), preferred_element_type=F32) for t in _bf_terms(x))


def _head_ones(width, head):
    r = lax.broadcasted_iota(jnp.int32, (width, width), 0) // head
    c = lax.broadcasted_iota(jnp.int32, (width, width), 1) // head
    return (r == c).astype(BF16)


def _rms(x):
    return x * lax.rsqrt(jnp.mean(x * x, axis=-1, keepdims=True) + EPS)


def _round_robin(gens):
    live = list(gens)
    while live:
        nxt = []
        for gen in live:
            try:
                next(gen)
                nxt.append(gen)
            except StopIteration:
                pass
        live = nxt


class _Group:
    def __init__(self, name, row0, batch, seq, pos0, has_state):
        self.name, self.row0, self.batch, self.seq, self.pos0, self.has_state = name, row0, batch, seq, pos0, has_state
        self.rows = batch * seq


PROMPT = _Group("prompt", 0, BATCH, SEQ, 0, False)
SAMPLE = _Group("sample", N_PROMPT, DEC_BATCH, DEC_SEQ, PAST_LEN, True)


def _pcall(body, *, name, grid, in_specs, args, out_specs, out_shape, into=None, scratch=(), semantics,
           vmem_mb=None):
    in_specs, args = list(in_specs), list(args)
    n_in, aliases = len(args), {}
    for k, arr in enumerate(into or ()):
        if arr is not None:
            aliases[len(args)] = k
            in_specs.append(pl.BlockSpec(memory_space=pl.ANY))
            args.append(arr)
    n_alias = len(aliases)

    def kernel_fn(*refs):
        return body(*refs[:n_in], *refs[n_in + n_alias:])

    params = dict(dimension_semantics=semantics)
    if vmem_mb is not None:
        params['vmem_limit_bytes'] = vmem_mb << 20
    return pl.pallas_call(
        kernel_fn, grid=grid, in_specs=in_specs, out_specs=out_specs, out_shape=out_shape,
        scratch_shapes=list(scratch), input_output_aliases=aliases,
        compiler_params=pltpu.CompilerParams(**params), name=name)(*args)


def _layer_block(layer, shape, index_map):
    return pl.BlockSpec((None,) + tuple(shape), lambda *g: (layer,) + tuple(index_map(*g)))


D_IN = 3 * BR_W + (3 * BR_W + 2 * RW_LORA) + 3 * BR_W + N_BRANCH * BR_W + N_BRANCH * D_MODEL
PACK_ROWS = 128
PACK_COLS = 2048


def _pack_kernel(w_ref, o_ref):
    o_rw = 3 * BR_W

    def copy(dst, src, n):
        for c in range(0, n, PACK_COLS):
            m = min(PACK_COLS, n - c)
            o_ref[:, dst + c:dst + c + m] = _bf(w_ref[src + c:src + c + m, :].T)

    def lora(src):
        t = w_ref[src:src + LORA_PAD, :].T
        keep = lax.broadcasted_iota(jnp.int32, t.shape, 1) < RW_LORA
        return _bf(jnp.where(keep, t, 0.0))

    copy(Z_HG, 0, o_rw + BR_W)
    copy(Z_RKV + BR_W, o_rw + BR_W + RW_LORA, 2 * BR_W)
    rest = o_rw + 3 * BR_W + 2 * RW_LORA
    copy(Z_RN, rest, D_IN - rest)
    o_ref[:, Z_LORA:Z_LORA + LORA_PAD] = lora(o_rw + BR_W)
    o_ref[:, Z_LORA + LORA_PAD:Z_USED] = lora(rest - RW_LORA)
    o_ref[:, Z_USED:] = jnp.zeros((PACK_ROWS, Z_W - Z_USED), BF16)


def _pack_w_in(w_in):
    return _pcall(
        _pack_kernel, name="pack_w_in", grid=(DEPTH, D_MODEL // PACK_ROWS),
        in_specs=[pl.BlockSpec((None, D_IN, PACK_ROWS), lambda l, i: (l, 0, i))], args=[jnp.swapaxes(w_in, 1, 2)],
        out_specs=pl.BlockSpec((None, PACK_ROWS, Z_W), lambda l, i: (l, i, 0)),
        out_shape=jax.ShapeDtypeStruct((DEPTH, D_MODEL, Z_W), BF16),
        semantics=("parallel", "parallel"), vmem_mb=48)


def _pack_shift_cols(a):
    r = a[..., :BR_W]
    wlo = a[..., BR_W:BR_W + RW_LORA]
    k = a[..., BR_W + RW_LORA:2 * BR_W + RW_LORA]
    v = a[..., 2 * BR_W + RW_LORA:3 * BR_W + RW_LORA]
    alo = a[..., 3 * BR_W + RW_LORA:]
    zl = jnp.zeros(a.shape[:-1] + (LORA_PAD - RW_LORA,), a.dtype)
    return jnp.concatenate([r, k, v], axis=-1), jnp.concatenate([wlo, zl, alo, zl], axis=-1)


def _unpack_shift_cols(rkv, lora):
    return jnp.concatenate([rkv[:, :BR_W], lora[:, :RW_LORA], rkv[:, BR_W:],
                            lora[:, LORA_PAD:LORA_PAD + RW_LORA]], axis=1)


def _row_pair_maps(tile):
    npt = N_PROMPT // tile
    return (lambda i, *_: (jnp.minimum(i, npt - 1), 0)), (lambda i, *_: (jnp.maximum(i - npt, 0), 0)), npt


def _inproj_kernel(xp_ref, xs_ref, g_ref, w_ref, o_ref, h_ref, *, npt):
    first_col = pl.program_id(1) == 0
    is_prompt = pl.program_id(0) < npt

    @pl.when(first_col & is_prompt)
    def _():
        h_ref[...] = _bf(_rms(xp_ref[...]) * g_ref[...])

    @pl.when(first_col & jnp.logical_not(is_prompt))
    def _():
        h_ref[...] = _bf(_rms(xs_ref[...]) * g_ref[...])

    o_ref[...] = jnp.dot(h_ref[...], w_ref[...], preferred_element_type=F32)


def _inproj(x_pair, g, w, layer):
    pmap, smap, npt = _row_pair_maps(INPROJ_TM)
    return _pcall(
        functools.partial(_inproj_kernel, npt=npt), name="inproj", grid=(N_ROWS // INPROJ_TM, Z_W // INPROJ_TN),
        in_specs=[pl.BlockSpec((INPROJ_TM, D_MODEL), pmap),
                  pl.BlockSpec((INPROJ_TM, D_MODEL), smap, pipeline_mode=pl.Buffered(1)),
                  _layer_block(layer, (1, D_MODEL), lambda i, j: (0, 0)),
                  _layer_block(layer, (D_MODEL, INPROJ_TN), lambda i, j: (0, j))],
        args=[*x_pair, g, w],
        out_specs=pl.BlockSpec((INPROJ_TM, INPROJ_TN), lambda i, j: (i, j)),
        out_shape=jax.ShapeDtypeStruct((N_ROWS, Z_W), F32),
        scratch=[pltpu.VMEM((INPROJ_TM, D_MODEL), BF16)],
        semantics=("parallel", "arbitrary"), vmem_mb=56)


def _gla_level_matrix(c):
    nlv = c.bit_length() - 1
    m = np.zeros(((nlv + 1) * c, c), np.float32)
    for r in range(c):
        m[r, :r + 1] = 1.0
        for p in range(nlv):
            bd = ((r >> (p + 1)) << (p + 1)) + (1 << p) - 1
            if (r >> p) & 1:
                m[(p + 1) * c + r, bd + 1:r + 1] = 1.0
            else:
                m[(p + 1) * c + r, r + 1:bd + 1] = 1.0
    return m


def _hgrn_kernel(*refs, c, nb, layer, has_state):
    if has_state:
        hq_ref, hf_ref, hi_ref, lbp_ref, gn_ref, ms_ref, s0_ref, y_ref, so_ref, s_scr = refs
    else:
        hq_ref, hf_ref, hi_ref, lbp_ref, gn_ref, ms_ref, y_ref, so_ref, s_scr = refs
    nlv = c.bit_length() - 1
    ci = pl.program_id(1)

    @pl.when(ci == 0)
    def _():
        if has_state:
            s_scr[...] = s0_ref[...]
        else:
            s_scr[...] = jnp.zeros_like(s_scr)

    lbp = lbp_ref[...]
    e = jnp.exp(lbp - jnp.max(lbp, axis=0, keepdims=True))
    sm = e / jnp.sum(e, axis=0, keepdims=True)
    lb = jnp.sum(sm[0:layer + 1], axis=0, keepdims=True) - sm[0:1]

    ti = lax.broadcasted_iota(jnp.int32, (c, c), 0)
    si = lax.broadcasted_iota(jnp.int32, (c, c), 1)
    txs = ti ^ si
    lower = ti > si
    ones_c = jnp.ones((c, LANES), F32)

    for ib in range(nb):
        rows = pl.ds(ib * c, c)
        hq, hf, v = hq_ref[rows, :], hf_ref[rows, :], hi_ref[rows, :]
        sig = jax.nn.sigmoid(hf)
        f_gate = lb + (1.0 - lb) * sig
        logf = jnp.log(jnp.maximum(f_gate, F_MIN))
        k = (1.0 - lb) * (1.0 - sig)
        q = jax.nn.silu(hq) * HG_DK ** -0.5
        lv = _sum_dot(ms_ref[...], logf)
        bl_col = _sum_dot_tn(logf, ones_c)
        qk = q * k

        def head(h):
            hs = slice(h * HG_DK, (h + 1) * HG_DK)
            qh, kh, vh, b = q[:, hs], k[:, hs], v[:, hs], lv[0:c, hs]
            att = jnp.where(ti == si, jnp.sum(qk[:, hs], axis=-1, keepdims=True), 0.0)
            for p in range(nlv):
                ex = jnp.exp(lv[(p + 1) * c:(p + 2) * c, hs])
                att = att + jnp.where(((txs >> p) == 1) & lower, _dot_nt(qh * ex, kh * ex), 0.0)
            yield
            s = s_scr[ib, h]
            o = _dot(att, vh) + _dot(qh * jnp.exp(b), s)
            bl = b[c - 1:c]
            s_scr[ib, h] = jnp.exp(bl_col[hs, :]) * s + _dot_tn(kh * jnp.exp(bl - b), vh)
            yield
            y_ref[rows, hs] = _rms(o) * gn_ref[...]

        _round_robin([head(h) for h in range(HG_HEADS)])

    @pl.when(ci == pl.num_programs(1) - 1)
    def _():
        so_ref[...] = s_scr[...]


def _hgrn(z, grp, layer, lower_bounds, gnorm, state, into):
    c = min(CHUNK, grp.seq)
    nb = 1 if grp.seq > c else 2
    nchunk = grp.seq // c
    blk = nb * c
    base = grp.row0 // blk

    def zspec(col0):
        return pl.BlockSpec((blk, BR_W), lambda i, ci: (base + i * nchunk + ci, col0 // BR_W))

    in_specs = [zspec(Z_HG), zspec(Z_HG + BR_W), zspec(Z_HG + 2 * BR_W),
                pl.BlockSpec((DEPTH, BR_W), lambda i, ci: (0, 0)),
                _layer_block(layer, (1, HG_DV), lambda i, ci: (0, 0)),
                pl.BlockSpec(((c.bit_length()) * c, c), lambda i, ci: (0, 0))]
    args = [z, z, z, lower_bounds, gnorm, jnp.asarray(_gla_level_matrix(c))]
    st_spec = _layer_block(layer, (nb, HG_HEADS, HG_DK, HG_DV), lambda i, ci: (i, 0, 0, 0))
    if grp.has_state:
        in_specs.append(st_spec)
        args.append(state)
    return _pcall(
        functools.partial(_hgrn_kernel, c=c, nb=nb, layer=layer, has_state=grp.has_state),
        name="hgrn_" + grp.name, grid=(grp.batch // nb, nchunk), in_specs=in_specs, args=args,
        out_specs=[pl.BlockSpec((blk, BR_W), lambda i, ci: (base + i * nchunk + ci, 0)), st_spec],
        out_shape=[jax.ShapeDtypeStruct((N_ROWS, BR_W), F32),
                   jax.ShapeDtypeStruct((DEPTH, grp.batch, HG_HEADS, HG_DK, HG_DV), F32)],
        into=into, scratch=[pltpu.VMEM((nb, HG_HEADS, HG_DK, HG_DV), F32)],
        semantics=("parallel", "arbitrary"), vmem_mb=48)


def _rope_kernel(inv_ref, cos_ref, sin_ref, *, pos0):
    t = cos_ref.shape[0]
    pos = (pos0 + lax.broadcasted_iota(jnp.int32, (t, RN_DK // 2), 0)).astype(F32)
    ang = pos * inv_ref[...]
    cos_ref[...] = jnp.cos(ang)
    sin_ref[...] = jnp.sin(ang)


def _rope_tables(grp):
    inv = ROPE_BASE ** (-jnp.arange(0, RN_DK, 2, dtype=F32) / RN_DK)
    shp = jax.ShapeDtypeStruct((grp.seq, RN_DK // 2), F32)
    return pl.pallas_call(functools.partial(_rope_kernel, pos0=grp.pos0), out_shape=[shp, shp],
                          name="rope")(inv[None, :])


def _ret_kernel(*refs, c, nb, has_state):
    if has_state:
        (q_ref, k_ref, v_ref, cos_ref, sin_ref, dm_ref, qd_ref, kd_ref, cd_ref, s0_ref,
         y_ref, so_ref, s_scr) = refs
    else:
        (q_ref, k_ref, v_ref, cos_ref, sin_ref, dm_ref, qd_ref, kd_ref, cd_ref,
         y_ref, so_ref, s_scr) = refs
    ci = pl.program_id(1)

    @pl.when(ci == 0)
    def _():
        if has_state:
            s_scr[...] = s0_ref[...]
        else:
            s_scr[...] = jnp.zeros_like(s_scr)

    cos, sin = cos_ref[...], sin_ref[...]
    half = RN_DK // 2

    def rot(x):
        x1, x2 = x[:, :half], x[:, half:]
        return jnp.concatenate([x1 * cos - x2 * sin, x1 * sin + x2 * cos], axis=1)

    def head(ib, h):
        rows, hs = pl.ds(ib * c, c), slice(h * RN_DK, (h + 1) * RN_DK)
        q = rot(q_ref[rows, hs])
        k = rot(k_ref[rows, hs]) * RN_DK ** -0.5
        v = v_ref[rows, hs]
        s = s_scr[ib, h]
        att = _dot_nt(q, k) * dm_ref[h]
        qs = _dot(q, s)
        s_scr[ib, h] = cd_ref[h] * s + _dot_tn(k * kd_ref[h], v)
        yield
        o = _dot(att, v) + qs * qd_ref[h]
        yield
        y_ref[rows, hs] = _rms(o)

    _round_robin([head(ib, h) for ib in range(nb) for h in range(RN_HEADS)])

    @pl.when(ci == pl.num_programs(1) - 1)
    def _():
        so_ref[...] = s_scr[...]


def _retention(z, grp, tables, state, layer, into):
    c = min(CHUNK, grp.seq)
    nb = 1 if grp.seq > c else 2
    nchunk = grp.seq // c
    blk = nb * c
    base = grp.row0 // blk
    log_gamma = jnp.log(1.0 - 2.0 ** (-5.0 - jnp.arange(RN_HEADS, dtype=F32)))
    j = jnp.arange(c, dtype=F32)
    rel = j[:, None] - j[None, :]
    dmat = jnp.where(rel >= 0, jnp.exp(log_gamma[:, None, None] * jnp.maximum(rel, 0.0)), 0.0)
    q_dec = jnp.exp(log_gamma[:, None] * (j + 1.0))[..., None]
    k_dec = jnp.exp(log_gamma[:, None] * (c - 1.0 - j))[..., None]
    c_dec = jnp.exp(log_gamma * c)[:, None, None]

    def zspec(col0):
        return pl.BlockSpec((blk, BR_W), lambda i, ci: (base + i * nchunk + ci, col0 // BR_W))

    def const(shape):
        return pl.BlockSpec(shape, lambda i, ci: (0,) * len(shape))

    tspec = pl.BlockSpec((c, RN_DK // 2), lambda i, ci: (ci, 0))
    in_specs = [zspec(Z_RN), zspec(Z_RN + BR_W), zspec(Z_RN + 2 * BR_W), tspec, tspec,
                const((RN_HEADS, c, c)), const((RN_HEADS, c, 1)), const((RN_HEADS, c, 1)), const((RN_HEADS, 1, 1))]
    args = [z, z, z, tables[0], tables[1], dmat, q_dec, k_dec, c_dec]
    st_spec = _layer_block(layer, (nb, RN_HEADS, RN_DK, RN_DV), lambda i, ci: (i, 0, 0, 0))
    if grp.has_state:
        in_specs.append(st_spec)
        args.append(state)
    return _pcall(
        functools.partial(_ret_kernel, c=c, nb=nb, has_state=grp.has_state),
        name="ret_" + grp.name, grid=(grp.batch // nb, nchunk), in_specs=in_specs, args=args,
        out_specs=[pl.BlockSpec((blk, BR_W), lambda i, ci: (base + i * nchunk + ci, 0)), st_spec],
        out_shape=[jax.ShapeDtypeStruct((N_ROWS, BR_W), F32),
                   jax.ShapeDtypeStruct((DEPTH, grp.batch, RN_HEADS, RN_DK, RN_DV), F32)],
        into=into, scratch=[pltpu.VMEM((nb, RN_HEADS, RN_DK, RN_DV), F32)],
        semantics=("parallel", "arbitrary"), vmem_mb=48)


RW_PREP_ROWS = 256


def _rw_token_mix(ur, ul, w0_ref, w2_ref, a0_ref, a2_ref, kk_ref, ka_ref, vmix):
    r, k, v = ur[:, :BR_W], ur[:, BR_W:2 * BR_W], ur[:, 2 * BR_W:]
    wlo, alo = ul[:, :LORA_PAD], ul[:, LORA_PAD:]
    w = -jax.nn.softplus(-(w0_ref[...] + _dot(jnp.tanh(wlo), w2_ref[...]))) - 0.5
    a = jax.nn.sigmoid(a0_ref[...] + _dot(alo, a2_ref[...]))
    kk = k * kk_ref[...]
    ones = _head_ones(MXU_DIM, RW_N)
    kk2 = kk * kk
    nrm2 = jnp.concatenate(
        [_split_dot(kk2[:, g * MXU_DIM:(g + 1) * MXU_DIM], ones) for g in range(BR_W // MXU_DIM)], axis=1)
    kk = kk / jnp.maximum(jnp.sqrt(nrm2), 1e-12)
    k = k * (1.0 + (a - 1.0) * ka_ref[...])
    if vmix is not None:
        v0_ref, v1_ref, v2_ref, vf_ref = vmix
        mix = jax.nn.sigmoid(v0_ref[...] + _dot(_dot(v, v1_ref[...]), v2_ref[...]))
        v = v + (vf_ref[...] - v) * mix
    return r, -jnp.exp(w), k, v, -kk, kk * a


def _rw_prep_kernel(*refs, tt, nb, has_state, with_vmix):
    refs = list(refs)
    zr_ref, zl_ref = refs[:2]
    pos = 2
    if has_state:
        sr_ref, sl_ref = refs[pos:pos + 2]
        pos += 2
    (mur_ref, mul_ref, w0_ref, w2_ref, a0_ref, a2_ref, kk_ref, ka_ref) = refs[pos:pos + 8]
    pos += 8
    if with_vmix:
        v0_ref, v1_ref, v2_ref, vf_ref = refs[pos:pos + 4]
        pos += 4
    r_ref, d_ref, k_ref, v_ref, a_ref, b_ref, lastr_ref, lastl_ref = refs[pos:pos + 8]
    pr_scr, pl_scr = refs[pos + 8:pos + 10]
    ti = pl.program_id(1)
    rows = nb * tt

    zr, zl = zr_ref[...], zl_ref[...]
    for ib in range(nb):
        lastr_ref[ib] = zr_ref[pl.ds(ib * tt + tt - 1, 1), :]
        lastl_ref[ib] = zl_ref[pl.ds(ib * tt + tt - 1, 1), :]
    first = (lax.broadcasted_iota(jnp.int32, (rows, 1), 0) % tt) == 0

    def prev_rows(z, st_ref, scr):
        w = z.shape[1]
        if nb > 1:
            st = st_ref[...]
            return jnp.broadcast_to(st[:, None, :], (nb, tt, w)).reshape(rows, w)

        @pl.when(ti == 0)
        def _():
            scr[0:1, :] = st_ref[...] if has_state else jnp.zeros((1, w), F32)

        return scr[0:1, :]

    def shifted(z, st_ref, scr, mu):
        prev = jnp.where(first, prev_rows(z, st_ref, scr), pltpu.roll(z, 1, 0))
        scr[0:1, :] = z[rows - 1:rows, :]
        return z + (prev - z) * mu

    ur = shifted(zr, sr_ref if has_state else None, pr_scr, mur_ref[...])
    ul = shifted(zl, sl_ref if has_state else None, pl_scr, mul_ref[...])
    vmix = (v0_ref, v1_ref, v2_ref, vf_ref) if with_vmix else None
    outs = _rw_token_mix(ur, ul, w0_ref, w2_ref, a0_ref, a2_ref, kk_ref, ka_ref, vmix)
    for ref, val in zip((r_ref, d_ref, k_ref, v_ref, a_ref, b_ref), outs):
        ref[...] = val


def _rw_prep(z, grp, prm, state_rkv, state_lora, v_first):
    tt = min(RW_PREP_ROWS, grp.seq)
    nb = RW_PREP_ROWS // tt
    ntile = grp.seq // tt
    rows = nb * tt
    base = grp.row0 // rows
    with_vmix = v_first is not None
    w_rkv, w_lora = 3 * BR_W, 2 * LORA_PAD

    def rowmap(i, t):
        return base + i * ntile + t

    def const(shape):
        return pl.BlockSpec(shape, lambda i, t: (0,) * len(shape))

    in_specs = [pl.BlockSpec((rows, w_rkv), lambda i, t: (rowmap(i, t), Z_RKV // w_rkv)),
                pl.BlockSpec((rows, w_lora), lambda i, t: (rowmap(i, t), Z_LORA // w_lora))]
    args = [z, z]
    if grp.has_state:
        in_specs += [pl.BlockSpec((nb, w_rkv), lambda i, t: (i, 0)),
                     pl.BlockSpec((nb, w_lora), lambda i, t: (i, 0))]
        args += [state_rkv, state_lora]
    in_specs += [const((1, w_rkv)), const((1, w_lora)), const((1, BR_W)), const((LORA_PAD, BR_W)),
                 const((1, BR_W)), const((LORA_PAD, BR_W)), const((1, BR_W)), const((1, BR_W))]
    args += [prm['mu_rkv'], prm['mu_lora'], prm['w0'], prm['w2'], prm['a0'], prm['a2'], prm['k_k'], prm['k_a']]
    out_spec = pl.BlockSpec((rows, BR_W), lambda i, t: (i * ntile + t, 0))
    if with_vmix:
        in_specs += [const((1, BR_W)), const((BR_W, LANES)), const((LANES, BR_W)), out_spec]
        args += [prm['v0'], prm['v1'], prm['v2'], v_first]
    out = jax.ShapeDtypeStruct((grp.rows, BR_W), F32)

    def last_spec(w):
        return pl.BlockSpec((nb, 1, w), lambda i, t: (i, 0, 0))

    def last_shape(w):
        return jax.ShapeDtypeStruct((grp.batch, 1, w), F32)

    return pl.pallas_call(
        functools.partial(_rw_prep_kernel, tt=tt, nb=nb, has_state=grp.has_state, with_vmix=with_vmix),
        grid=(grp.batch // nb, ntile),
        in_specs=in_specs,
        out_specs=[out_spec] * 6 + [last_spec(w_rkv), last_spec(w_lora)],
        out_shape=[out] * 6 + [last_shape(w_rkv), last_shape(w_lora)],
        scratch_shapes=[pltpu.VMEM((SUBLANES, w_rkv), F32), pltpu.VMEM((SUBLANES, w_lora), F32)],
        compiler_params=pltpu.CompilerParams(
            dimension_semantics=("parallel", "arbitrary"), vmem_limit_bytes=48 << 20),
        name="rw_prep_" + ("sample" if grp.has_state else "prompt"),
    )(*args)


RW_TILE = MXU_DIM
RW_TILE_HEADS = RW_TILE // RW_N
RW_LANE_HEADS = LANES // RW_N
RW_SCAN_UNROLL = 4


def _rw_scan_kernel(r_ref, d_ref, k_ref, v_ref, a_ref, b_ref, lw_ref, lb_ref, rk_ref, s0_ref,
                    y_ref, so_ref, s_scr, v_scr, y_scr):
    t = pl.program_id(1)

    @pl.when(t == 0)
    def _():
        s_scr[...] = s0_ref[...]

    r, ld, k, a, b = (ref[...].T for ref in (r_ref, d_ref, k_ref, a_ref, b_ref))
    v_scr[...] = v_ref[...].T
    w = jnp.exp(ld)

    for h in range(RW_LANE_HEADS):
        hs = slice(h * RW_N, (h + 1) * RW_N)
        rh, wh, kh, ah, bh = r[hs], w[hs], k[hs], a[hs], b[hs]

        def row_update(i, carry, h=h, rh=rh, wh=wh, kh=kh, ah=ah, bh=bh):
            s = s_scr[h, i]
            sa = jnp.sum(s * ah, axis=0, keepdims=True)
            s = s * wh + sa * bh + v_scr[pl.ds(h * RW_N + i, 1), :] * kh
            s_scr[h, i] = s
            y_scr[pl.ds(h * RW_N + i, 1), :] = jnp.sum(s * rh, axis=0, keepdims=True)
            return carry

        lax.fori_loop(0, RW_N, row_update, 0, unroll=RW_SCAN_UNROLL)

        y, vh = y_scr[hs, :], v_scr[hs, :]
        mu = jnp.mean(y, axis=0, keepdims=True)
        dlt = y - mu
        var = jnp.mean(dlt * dlt, axis=0, keepdims=True)
        yn = dlt * lax.rsqrt(var + RW_GN_EPS) * lw_ref[h] + lb_ref[h]
        bonus = jnp.sum(rh * kh * rk_ref[h], axis=0, keepdims=True)
        y_scr[hs, :] = yn + bonus * vh

    y_ref[...] = y_scr[...].T

    @pl.when(t == pl.num_programs(1) - 1)
    def _():
        so_ref[...] = s_scr[...]


def _rw_scan(seqs, grp, norm, state_t, layer, into):
    assert grp.has_state and grp.batch == LANES
    hp = RW_LANE_HEADS
    width = hp * RW_N
    tm = jnp.stack(seqs).reshape(len(seqs), grp.batch, grp.seq, BR_W).transpose(0, 2, 1, 3)
    seq_specs = [pl.BlockSpec((None, None, grp.batch, width), lambda j, t, n=n: (n, t, 0, j))
                 for n in range(len(seqs))]
    par_spec = pl.BlockSpec((hp, RW_N, 1), lambda j, t: (j, 0, 0))
    st_spec = _layer_block(layer, (hp, RW_N, RW_N, grp.batch), lambda j, t: (j, 0, 0, 0))
    y, st = _pcall(
        _rw_scan_kernel, name="rw_scan_" + grp.name, grid=(RW_HEADS // hp, grp.seq),
        in_specs=seq_specs + [par_spec] * 3 + [st_spec],
        args=[tm] * len(seqs) + [p.reshape(RW_HEADS, RW_N, 1) for p in norm] + [state_t],
        out_specs=[pl.BlockSpec((None, grp.batch, width), lambda j, t: (t, 0, j)), st_spec],
        out_shape=[jax.ShapeDtypeStruct((grp.seq, grp.batch, BR_W), F32),
                   jax.ShapeDtypeStruct((DEPTH, RW_HEADS, RW_N, RW_N, grp.batch), F32)],
        into=(None, into[1]),
        scratch=[pltpu.VMEM((hp, RW_N, RW_N, grp.batch), F32), pltpu.VMEM((width, grp.batch), F32),
                 pltpu.VMEM((width, grp.batch), F32)],
        semantics=("parallel", "arbitrary"), vmem_mb=48)
    y_rows = y.transpose(1, 0, 2).reshape(grp.rows, BR_W)
    return lax.dynamic_update_slice(into[0], y_rows, (grp.row0, 0)), st


RW_CHUNK = RW_N
RW_CHUNKS_PER_STEP = 2
RW_SOLVE_BASE = 8


def _rw_chunk_kernel(*refs, with_vmix):
    refs = list(refs)
    zr_ref, zl_ref, mur_ref, mul_ref, w0_ref, w2_ref, a0_ref, a2_ref, kk_ref, ka_ref = refs[:10]
    pos = 10
    vmix = None
    if with_vmix:
        vmix = tuple(refs[pos:pos + 4])
        pos += 4
    lw_ref, lb_ref, rk_ref, y_ref, ho_ref, vout_ref, lastr_ref, lastl_ref = refs[pos:pos + 8]
    h_scr, seq_scr, pr_scr, pl_scr = refs[pos + 8:pos + 12]
    ci = pl.program_id(1)
    ln, tw, nt = RW_CHUNK, RW_TILE, BR_W // RW_TILE
    nrows = ln * RW_CHUNKS_PER_STEP

    @pl.when(ci == 0)
    def _():
        h_scr[...] = jnp.zeros_like(h_scr)
        pr_scr[...] = jnp.zeros_like(pr_scr)
        pl_scr[...] = jnp.zeros_like(pl_scr)

    first = lax.broadcasted_iota(jnp.int32, (nrows, 1), 0) == 0

    def shifted(z_ref, scr, mu, last_ref):
        z = z_ref[...]
        prev = jnp.where(first, scr[0:1, :], pltpu.roll(z, 1, 0))
        scr[0:1, :] = z[nrows - 1:nrows, :]
        last_ref[0] = z[nrows - 1:nrows, :]
        return z + (prev - z) * mu

    ur = shifted(zr_ref, pr_scr, mur_ref[...], lastr_ref)
    ul = shifted(zl_ref, pl_scr, mul_ref[...], lastl_ref)
    for n, val in enumerate(_rw_token_mix(ur, ul, w0_ref, w2_ref, a0_ref, a2_ref, kk_ref, ka_ref, vmix)):
        seq_scr[n] = val
    vout_ref[...] = seq_scr[3]

    row = lax.broadcasted_iota(jnp.int32, (tw, tw), 0)
    col = lax.broadcasted_iota(jnp.int32, (tw, tw), 1)
    same_head = (row // RW_N) == (col // RW_N)
    ones = same_head.astype(BF16)
    t_i = lax.broadcasted_iota(jnp.int32, (ln, tw), 0)
    s_i = lax.broadcasted_iota(jnp.int32, (ln, tw), 1) % ln
    strict, incl = s_i < t_i, s_i <= t_i
    tril = (lax.broadcasted_iota(jnp.int32, (ln, ln), 0) >= lax.broadcasted_iota(jnp.int32, (ln, ln), 1)).astype(F32)

    def blocks(x):
        return jnp.where(same_head, jnp.concatenate([x] * RW_TILE_HEADS, axis=0), 0.0).astype(BF16)

    def mm(x, w):
        return jnp.dot(_bf(x), w, preferred_element_type=F32)

    def prepare(j, g, out):
        rows, sl = pl.ds(j * ln, ln), slice(g * tw, (g + 1) * tw)
        r, ld, k, v, a, b = (seq_scr[n, rows, sl] for n in range(6))
        c = _sum_dot(tril, ld)
        tot_col = _sum_dot_tn(ld, jnp.ones((ln, LANES), F32))
        yield
        c_last = c[ln - 1:ln]
        e_inv, e_end = jnp.exp(-c), jnp.exp(c_last - c)
        at, rt = a * jnp.exp(c - ld), r * jnp.exp(c)
        bt, kt, bp, kp = b * e_inv, k * e_inv, b * e_end, k * e_end
        gram = lax.dot_general(_bf(jnp.concatenate([at, rt], axis=0)),
                               jnp.concatenate([blocks(bt), blocks(kt)], axis=0),
                               (((1,), (1,)), ((), ())), preferred_element_type=F32)
        yield
        n_ab = jnp.where(strict, gram[:ln, :tw], 0.0)
        a_ak = jnp.where(strict, gram[:ln, tw:], 0.0)
        a_rb = jnp.where(incl, gram[ln:, :tw], 0.0)
        a_rk = jnp.where(incl, gram[ln:, tw:], 0.0)
        av = mm(jnp.concatenate([a_ak, a_rk], axis=0), blocks(v))
        yield
        def same_block(m):
            return (t_i // m) == (s_i // m)

        nd = jnp.where(same_block(RW_SOLVE_BASE), n_ab, 0.0)
        tinv = jnp.where(s_i == t_i, 1.0, 0.0) + nd
        p = mm(nd, blocks(nd))
        yield
        m = 2
        while m < RW_SOLVE_BASE:
            m *= 2
            if m < RW_SOLVE_BASE:
                both = mm(jnp.concatenate([tinv, p], axis=0), blocks(p))
                tinv, p = tinv + both[:ln], both[ln:]
            else:
                tinv = tinv + mm(tinv, blocks(p))
            yield
        m = RW_SOLVE_BASE
        while m < ln:
            lower_left = jnp.where(same_block(2 * m) & jnp.logical_not(same_block(m)), n_ab, 0.0)
            tn = mm(tinv, blocks(lower_left))
            yield
            tinv = tinv + mm(tn, blocks(tinv))
            yield
            m *= 2
        w = mm(tinv, jnp.concatenate([blocks(at), blocks(av[:ln])], axis=1))
        yield
        at2, uv = w[:, :tw], w[:, tw:]
        m_off = jnp.where(same_head, _dot_tn(bp, at2), 0.0)
        cc = jnp.where(same_head, _dot_tn(jnp.concatenate([bp, kp], axis=0), jnp.concatenate([uv, v], axis=0)), 0.0)
        qy = mm(a_rb, jnp.concatenate([blocks(at2), blocks(uv)], axis=1))
        decay_col = jnp.concatenate([jnp.exp(tot_col)] * (tw // LANES), axis=1)
        out[j, g] = (rt + qy[:, :tw], qy[:, tw:] + av[ln:], m_off, cc, decay_col)

    items = [(j, g) for j in range(RW_CHUNKS_PER_STEP) for g in range(nt)]
    prepared = {}
    _round_robin([prepare(j, g, prepared) for j, g in items])

    for j, g in items:
        q, yc, m_off, cc, decay_col = prepared[(j, g)]
        rows, sl = pl.ds(j * ln, ln), slice(g * tw, (g + 1) * tw)
        h = h_scr[g]
        qm = mm(jnp.concatenate([q, m_off], axis=0), _bf(h))
        y = qm[:ln] + yc
        h_scr[g] = decay_col * h + qm[ln:] + cc

        mu = _split_dot(y, ones) * (1.0 / RW_N)
        dlt = y - mu
        var = _split_dot(dlt * dlt, ones) * (1.0 / RW_N)
        yn = dlt * lax.rsqrt(var + RW_GN_EPS) * lw_ref[:, sl] + lb_ref[:, sl]
        bonus = _split_dot(seq_scr[0, rows, sl] * seq_scr[2, rows, sl] * rk_ref[:, sl], ones)
        y_ref[rows, sl] = yn + bonus * seq_scr[3, rows, sl]

    @pl.when(ci == pl.num_programs(1) - 1)
    def _():
        for g in range(nt):
            h = h_scr[g]
            for hd in range(RW_TILE_HEADS):
                ho_ref[0, g * RW_TILE_HEADS + hd] = h[hd * RW_N:(hd + 1) * RW_N, hd * RW_N:(hd + 1) * RW_N]


def _rw_scan_chunked(z, grp, prm, norm, v_first, layer, into):
    blk = RW_CHUNK * RW_CHUNKS_PER_STEP
    assert not grp.has_state and grp.seq % blk == 0 and grp.row0 % blk == 0
    nstep = grp.seq // blk
    base = grp.row0 // blk
    with_vmix = v_first is not None
    w_rkv, w_lora = 3 * BR_W, 2 * LORA_PAD

    def const(shape):
        return pl.BlockSpec(shape, lambda i, c: (0,) * len(shape))

    seq_spec = pl.BlockSpec((blk, BR_W), lambda i, c: (i * nstep + c, 0))
    in_specs = [pl.BlockSpec((blk, w_rkv), lambda i, c: (base + i * nstep + c, Z_RKV // w_rkv)),
                pl.BlockSpec((blk, w_lora), lambda i, c: (base + i * nstep + c, Z_LORA // w_lora)),
                const((1, w_rkv)), const((1, w_lora)), const((1, BR_W)), const((LORA_PAD, BR_W)),
                const((1, BR_W)), const((LORA_PAD, BR_W)), const((1, BR_W)), const((1, BR_W))]
    args = [z, z, prm['mu_rkv'], prm['mu_lora'], prm['w0'], prm['w2'], prm['a0'], prm['a2'], prm['k_k'], prm['k_a']]
    if with_vmix:
        in_specs += [const((1, BR_W)), const((BR_W, LANES)), const((LANES, BR_W)), seq_spec]
        args += [prm['v0'], prm['v1'], prm['v2'], v_first]
    in_specs += [const((1, BR_W))] * 3
    args += list(norm)

    def last_spec(w):
        return pl.BlockSpec((1, 1, w), lambda i, c: (i, 0, 0))

    return _pcall(
        functools.partial(_rw_chunk_kernel, with_vmix=with_vmix),
        name="rw_chunk_" + grp.name, grid=(grp.batch, nstep), in_specs=in_specs, args=args,
        out_specs=[pl.BlockSpec((blk, BR_W), lambda i, c: (base + i * nstep + c, 0)),
                   _layer_block(layer, (1, RW_HEADS, RW_N, RW_N), lambda i, c: (i, 0, 0, 0)),
                   seq_spec, last_spec(w_rkv), last_spec(w_lora)],
        out_shape=[jax.ShapeDtypeStruct((N_ROWS, BR_W), F32),
                   jax.ShapeDtypeStruct((DEPTH, grp.batch, RW_HEADS, RW_N, RW_N), F32),
                   jax.ShapeDtypeStruct((grp.rows, BR_W), F32),
                   jax.ShapeDtypeStruct((grp.batch, 1, w_rkv), F32),
                   jax.ShapeDtypeStruct((grp.batch, 1, w_lora), F32)],
        into=tuple(into) + (None, None, None),
        scratch=[pltpu.VMEM((BR_W // RW_TILE, RW_TILE, RW_TILE), F32), pltpu.VMEM((6, blk, BR_W), F32),
                 pltpu.VMEM((SUBLANES, w_rkv), F32), pltpu.VMEM((SUBLANES, w_lora), F32)],
        semantics=("parallel", "arbitrary"), vmem_mb=48)


MERGE_TM = 256


def _merge_kernel(y0_ref, y1_ref, y2_ref, g_ref, zm_ref, wb_ref, m_ref):
    acc = None
    for n, y_ref in enumerate((y0_ref, y1_ref, y2_ref)):
        ys = y_ref[...] * jax.nn.silu(g_ref[:, n * BR_W:(n + 1) * BR_W])
        proj = jnp.dot(_bf(ys), wb_ref[n], preferred_element_type=F32)
        term = jax.nn.sigmoid(zm_ref[:, n * D_MODEL:(n + 1) * D_MODEL]) * proj
        acc = term if acc is None else acc + term
    m_ref[...] = _bf(acc)


def _merge(ys, z, wb, layer):
    n = z.shape[0]
    yspec = pl.BlockSpec((MERGE_TM, BR_W), lambda i: (i, 0))
    gw, mw = N_BRANCH * BR_W, N_BRANCH * D_MODEL
    return _pcall(
        _merge_kernel, name="merge", grid=(n // MERGE_TM,),
        in_specs=[yspec, yspec, yspec,
                  pl.BlockSpec((MERGE_TM, gw), lambda i: (i, Z_GATE // gw)),
                  pl.BlockSpec((MERGE_TM, mw), lambda i: (i, Z_MERGE // mw)),
                  pl.BlockSpec((None, N_BRANCH, BR_W, D_MODEL), lambda i: (layer, 0, 0, 0),
                               pipeline_mode=pl.Buffered(1))],
        args=[*ys, z, z, wb],
        out_specs=pl.BlockSpec((MERGE_TM, D_MODEL), lambda i: (i, 0)),
        out_shape=jax.ShapeDtypeStruct((n, D_MODEL), BF16),
        semantics=("parallel",), vmem_mb=48)


POST_TM = 256


def _post_kernel(xp_ref, xs_ref, m_ref, pp_ref, ps_ref, wo_ref, np_ref, pu_ref, pn_ref, pg_ref, op_ref, os_ref,
                 *, npt):
    is_prompt = pl.program_id(0) < npt
    x = jnp.where(is_prompt, xp_ref[...], xs_ref[...])
    p = jnp.where(is_prompt, pp_ref[...], ps_ref[...])
    x = x + _rms(jnp.dot(m_ref[...], wo_ref[...], preferred_element_type=F32)) * np_ref[...]
    e = _rms(jnp.dot(_bf(p), pu_ref[...], preferred_element_type=F32)) * pn_ref[...]
    out = x + e * jax.nn.sigmoid(jnp.dot(_bf(x), pg_ref[...], preferred_element_type=F32))

    @pl.when(is_prompt)
    def _():
        op_ref[...] = out

    @pl.when(jnp.logical_not(is_prompt))
    def _():
        os_ref[...] = out


def _post(x_pair, m, p_pair, wo, norm_post, ple_up, ple_norm, ple_gate, layer):
    pmap, smap, npt = _row_pair_maps(POST_TM)

    def const(shape):
        return pl.BlockSpec((None,) + shape, lambda i: (layer, 0, 0), pipeline_mode=pl.Buffered(1))

    def pair(w):
        return [pl.BlockSpec((POST_TM, w), pmap), pl.BlockSpec((POST_TM, w), smap)]

    ple_pair = [_layer_block(layer, (POST_TM, PLE_DIM), pmap), _layer_block(layer, (POST_TM, PLE_DIM), smap)]
    return _pcall(
        functools.partial(_post_kernel, npt=npt), name="post", grid=(N_ROWS // POST_TM,),
        in_specs=pair(D_MODEL) + [pl.BlockSpec((POST_TM, D_MODEL), lambda i: (i, 0))] + ple_pair
        + [const((D_MODEL, D_MODEL)), const((1, D_MODEL)), const((PLE_DIM, D_MODEL)),
           const((1, D_MODEL)), const((D_MODEL, D_MODEL))],
        args=[*x_pair, m, *p_pair, wo, norm_post, ple_up, ple_norm, ple_gate],
        out_specs=pair(D_MODEL),
        out_shape=[jax.ShapeDtypeStruct((N_PROMPT, D_MODEL), F32), jax.ShapeDtypeStruct((N_SAMPLE, D_MODEL), F32)],
        semantics=("arbitrary",), vmem_mb=48)


def _pad_rows(a, rows):
    return jnp.concatenate([a, jnp.zeros((rows - a.shape[0],) + a.shape[1:], a.dtype)], axis=0)


def _pad_cols(a, cols):
    return jnp.concatenate([a, jnp.zeros(a.shape[:-1] + (cols - a.shape[-1],), a.dtype)], axis=-1)


def _layer_rows(a):
    return a.reshape(a.shape[0], 1, a.shape[1])


def kernel(x_prompt, x_sample, state_hgrn, state_rwkv, state_shift, state_ret, p_prompt, p_sample,
           norm_pre, w_in, hg_lower_bounds, hg_norm, rw_mu, rw_w0, rw_w2, rw_a0, rw_a2, rw_k_k, rw_k_a,
           rw_v0, rw_v1, rw_v2, rw_r_k, rw_ln_w, rw_ln_b, w_branch, w_out, norm_post, ple_up, ple_norm, ple_gate):
    groups = (PROMPT, SAMPLE)
    x = (x_prompt.reshape(N_PROMPT, D_MODEL), x_sample.reshape(N_SAMPLE, D_MODEL))
    p = (p_prompt.reshape(DEPTH, N_PROMPT, PLE_DIM), p_sample.reshape(DEPTH, N_SAMPLE, PLE_DIM))
    w_packed = _pack_w_in(w_in)
    wb, wo, pu, pg = _bf(w_branch), _bf(w_out), _bf(ple_up), _bf(ple_gate)
    g_pre, g_hg, g_post, g_ple = (_layer_rows(a) for a in (norm_pre, hg_norm, norm_post, ple_norm))
    tables = [_rope_tables(g) for g in groups]
    state_rwkv_t = jnp.transpose(state_rwkv, (0, 2, 3, 4, 1))
    v_first = [None, None]
    shifts = [[], []]
    st_out = [[None, None, None] for _ in groups]

    for l in range(DEPTH):
        z = _inproj(x, g_pre, w_packed, l)
        mu_rkv, mu_lora = _pack_shift_cols(rw_mu[l][None])
        prm = dict(mu_rkv=mu_rkv, mu_lora=mu_lora, w0=rw_w0[l][None],
                   w2=_bf(_pad_rows(rw_w2[l], LORA_PAD)), a0=rw_a0[l][None],
                   a2=_bf(_pad_rows(rw_a2[l], LORA_PAD)), k_k=rw_k_k[l][None], k_a=rw_k_a[l][None])
        if l > 0:
            prm.update(v0=rw_v0[l - 1][None], v1=_bf(_pad_cols(rw_v1[l - 1], LANES)),
                       v2=_bf(_pad_rows(rw_v2[l - 1], LANES)))
        rw_norm = (rw_ln_w[l][None], rw_ln_b[l][None], rw_r_k[l].reshape(1, BR_W))
        ys = [None, None, None]
        for gi, grp in enumerate(groups):
            st_rkv, st_lora = _pack_shift_cols(state_shift[l]) if grp.has_state else (None, None)
            ys[0], st_out[gi][0] = _hgrn(z, grp, l, hg_lower_bounds, g_hg, state_hgrn, (ys[0], st_out[gi][0]))
            if grp.has_state:
                *seqs, last_rkv, last_lora = _rw_prep(z, grp, prm, st_rkv, st_lora, v_first[gi])
                ys[1], st_out[gi][1] = _rw_scan(seqs, grp, rw_norm, state_rwkv_t, l, (ys[1], st_out[gi][1]))
                v_rows = seqs[3]
            else:
                ys[1], st_out[gi][1], v_rows, last_rkv, last_lora = _rw_scan_chunked(
                    z, grp, prm, rw_norm, v_first[gi], l, (ys[1], st_out[gi][1]))
            shifts[gi].append(_unpack_shift_cols(last_rkv[:, 0], last_lora[:, 0]))
            if l == 0:
                v_first[gi] = v_rows
            ys[2], st_out[gi][2] = _retention(z, grp, tables[gi], state_ret, l, (ys[2], st_out[gi][2]))
        m = _merge(ys, z, wb, l)
        x = _post(x, m, p, wo, g_post, pu, g_ple, pg, l)

    y_prompt = x[0].reshape(BATCH, SEQ, D_MODEL)
    y_sample = x[1].reshape(DEC_BATCH, DEC_SEQ, D_MODEL)
    (hg_p, rw_p, rn_p), (hg_s, rw_s, rn_s) = st_out
    rw_p = jnp.swapaxes(rw_p, -1, -2)
    rw_s = jnp.transpose(rw_s, (0, 4, 1, 2, 3))
    sh_p, sh_s = (jnp.stack(s, axis=0) for s in shifts)
    return (y_prompt, y_sample, hg_p, hg_s, rw_p, rw_s, sh_p, sh_s, rn_p, rn_s)
```

```python
import functools

import numpy as np
import jax
import jax.numpy as jnp
from jax import lax
from jax.experimental import pallas as pl
from jax.experimental.pallas import tpu as pltpu

F32 = jnp.float32
BF16 = jnp.bfloat16

D_MODEL = 2048
BATCH, SEQ = 4, 2048
DEC_BATCH, DEC_SEQ = 128, 8
DEPTH = 2
PAST_LEN = 16384
N_BRANCH = 3
BR_W = 1024
HG_HEADS, HG_DK, HG_DV = 8, 128, 128
F_MIN = 1e-30
RW_HEADS, RW_N = 16, 64
RW_LORA = 96
RW_MV_LORA = 64
RW_GN_EPS = 64e-5
RN_HEADS, RN_DK, RN_DV = 4, 256, 256
ROPE_BASE = 10000.0
CHUNK = 64
PLE_DIM = 256
EPS = 1e-6

LANES = 128
SUBLANES = 8
MXU_DIM = 256

LORA_PAD = LANES
Z_HG = 0
Z_RKV = Z_HG + 3 * BR_W
Z_RN = Z_RKV + 3 * BR_W
Z_GATE = Z_RN + 3 * BR_W
Z_MERGE = Z_GATE + N_BRANCH * BR_W
Z_LORA = Z_MERGE + N_BRANCH * D_MODEL
Z_USED = Z_LORA + 2 * LORA_PAD
INPROJ_TN = 1024
Z_W = -(-Z_USED // INPROJ_TN) * INPROJ_TN
INPROJ_TM = 1024

N_PROMPT = BATCH * SEQ
N_SAMPLE = DEC_BATCH * DEC_SEQ
N_ROWS = N_PROMPT + N_SAMPLE


def _bf(x):
    return x.astype(BF16)


def _dot(a, b):
    return jnp.dot(_bf(a), _bf(b), preferred_element_type=F32)


def _dot_nt(a, b):
    return lax.dot_general(_bf(a), _bf(b), (((1,), (1,)), ((), ())), preferred_element_type=F32)


def _dot_tn(a, b):
    return lax.dot_general(_bf(a), _bf(b), (((0,), (0,)), ((), ())), preferred_element_type=F32)


def _split_dot(x, g):
    hi = _bf(x)
    lo = _bf(x - hi.astype(F32))
    return (jnp.dot(hi, g, preferred_element_type=F32) + jnp.dot(lo, g, preferred_element_type=F32))


def _bf_terms(x):
    hi = _bf(x)
    r1 = x - hi.astype(F32)
    mid = _bf(r1)
    return hi, mid, _bf(r1 - mid.astype(F32))


def _sum_dot(m, x, terms=3):
    m = _bf(m)
    return sum(jnp.dot(m, t, preferred_element_type=F32) for t in _bf_terms(x)[:terms])


def _sum_dot_tn(x, m):
    m = _bf(m)
    return sum(lax.dot_general(t, m, (((0,), (0,)), ((), ())), preferred_element_type=F32) for t in _bf_terms(x))


def _head_ones(width, head):
    r = lax.broadcasted_iota(jnp.int32, (width, width), 0) // head
    c = lax.broadcasted_iota(jnp.int32, (width, width), 1) // head
    return (r == c).astype(BF16)


def _rms(x):
    return x * lax.rsqrt(jnp.mean(x * x, axis=-1, keepdims=True) + EPS)


def _round_robin(gens):
    live = list(gens)
    while live:
        nxt = []
        for gen in live:
            try:
                next(gen)
                nxt.append(gen)
            except StopIteration:
                pass
        live = nxt


class _Group:
    def __init__(self, name, row0, batch, seq, pos0, has_state):
        self.name, self.row0, self.batch, self.seq, self.pos0, self.has_state = name, row0, batch, seq, pos0, has_state
        self.rows = batch * seq


PROMPT = _Group("prompt", 0, BATCH, SEQ, 0, False)
SAMPLE = _Group("sample", N_PROMPT, DEC_BATCH, DEC_SEQ, PAST_LEN, True)


def _pcall(body, *, name, grid, in_specs, args, out_specs, out_shape, into=None, scratch=(), semantics,
           vmem_mb=None):
    in_specs, args = list(in_specs), list(args)
    n_in, aliases = len(args), {}
    for k, arr in enumerate(into or ()):
        if arr is not None:
            aliases[len(args)] = k
            in_specs.append(pl.BlockSpec(memory_space=pl.ANY))
            args.append(arr)
    n_alias = len(aliases)

    def kernel_fn(*refs):
        return body(*refs[:n_in], *refs[n_in + n_alias:])

    params = dict(dimension_semantics=semantics)
    if vmem_mb is not None:
        params['vmem_limit_bytes'] = vmem_mb << 20
    return pl.pallas_call(
        kernel_fn, grid=grid, in_specs=in_specs, out_specs=out_specs, out_shape=out_shape,
        scratch_shapes=list(scratch), input_output_aliases=aliases,
        compiler_params=pltpu.CompilerParams(**params), name=name)(*args)


def _layer_block(layer, shape, index_map):
    return pl.BlockSpec((None,) + tuple(shape), lambda *g: (layer,) + tuple(index_map(*g)))


D_IN = 3 * BR_W + (3 * BR_W + 2 * RW_LORA) + 3 * BR_W + N_BRANCH * BR_W + N_BRANCH * D_MODEL
PACK_ROWS = 128
PACK_COLS = 2048


def _pack_kernel(w_ref, o_ref):
    o_rw = 3 * BR_W

    def copy(dst, src, n):
        for c in range(0, n, PACK_COLS):
            m = min(PACK_COLS, n - c)
            o_ref[:, dst + c:dst + c + m] = _bf(w_ref[src + c:src + c + m, :].T)

    def lora(src):
        t = w_ref[src:src + LORA_PAD, :].T
        keep = lax.broadcasted_iota(jnp.int32, t.shape, 1) < RW_LORA
        return _bf(jnp.where(keep, t, 0.0))

    copy(Z_HG, 0, o_rw + BR_W)
    copy(Z_RKV + BR_W, o_rw + BR_W + RW_LORA, 2 * BR_W)
    rest = o_rw + 3 * BR_W + 2 * RW_LORA
    copy(Z_RN, rest, D_IN - rest)
    o_ref[:, Z_LORA:Z_LORA + LORA_PAD] = lora(o_rw + BR_W)
    o_ref[:, Z_LORA + LORA_PAD:Z_USED] = lora(rest - RW_LORA)
    o_ref[:, Z_USED:] = jnp.zeros((PACK_ROWS, Z_W - Z_USED), BF16)


def _pack_w_in(w_in):
    return _pcall(
        _pack_kernel, name="pack_w_in", grid=(DEPTH, D_MODEL // PACK_ROWS),
        in_specs=[pl.BlockSpec((None, D_IN, PACK_ROWS), lambda l, i: (l, 0, i))], args=[jnp.swapaxes(w_in, 1, 2)],
        out_specs=pl.BlockSpec((None, PACK_ROWS, Z_W), lambda l, i: (l, i, 0)),
        out_shape=jax.ShapeDtypeStruct((DEPTH, D_MODEL, Z_W), BF16),
        semantics=("parallel", "parallel"), vmem_mb=48)


def _pack_shift_cols(a):
    r = a[..., :BR_W]
    wlo = a[..., BR_W:BR_W + RW_LORA]
    k = a[..., BR_W + RW_LORA:2 * BR_W + RW_LORA]
    v = a[..., 2 * BR_W + RW_LORA:3 * BR_W + RW_LORA]
    alo = a[..., 3 * BR_W + RW_LORA:]
    zl = jnp.zeros(a.shape[:-1] + (LORA_PAD - RW_LORA,), a.dtype)
    return jnp.concatenate([r, k, v], axis=-1), jnp.concatenate([wlo, zl, alo, zl], axis=-1)


def _unpack_shift_cols(rkv, lora):
    return jnp.concatenate([rkv[:, :BR_W], lora[:, :RW_LORA], rkv[:, BR_W:],
                            lora[:, LORA_PAD:LORA_PAD + RW_LORA]], axis=1)


def _row_pair_maps(tile):
    npt = N_PROMPT // tile
    return (lambda i, *_: (jnp.minimum(i, npt - 1), 0)), (lambda i, *_: (jnp.maximum(i - npt, 0), 0)), npt


def _inproj_kernel(xp_ref, xs_ref, g_ref, w_ref, o_ref, h_ref, *, npt):
    first_col = pl.program_id(1) == 0
    is_prompt = pl.program_id(0) < npt

    @pl.when(first_col & is_prompt)
    def _():
        h_ref[...] = _bf(_rms(xp_ref[...]) * g_ref[...])

    @pl.when(first_col & jnp.logical_not(is_prompt))
    def _():
        h_ref[...] = _bf(_rms(xs_ref[...]) * g_ref[...])

    o_ref[...] = jnp.dot(h_ref[...], w_ref[...], preferred_element_type=F32)


def _inproj(x_pair, g, w, layer):
    pmap, smap, npt = _row_pair_maps(INPROJ_TM)
    return _pcall(
        functools.partial(_inproj_kernel, npt=npt), name="inproj", grid=(N_ROWS // INPROJ_TM, Z_W // INPROJ_TN),
        in_specs=[pl.BlockSpec((INPROJ_TM, D_MODEL), pmap),
                  pl.BlockSpec((INPROJ_TM, D_MODEL), smap, pipeline_mode=pl.Buffered(1)),
                  _layer_block(layer, (1, D_MODEL), lambda i, j: (0, 0)),
                  _layer_block(layer, (D_MODEL, INPROJ_TN), lambda i, j: (0, j))],
        args=[*x_pair, g, w],
        out_specs=pl.BlockSpec((INPROJ_TM, INPROJ_TN), lambda i, j: (i, j)),
        out_shape=jax.ShapeDtypeStruct((N_ROWS, Z_W), F32),
        scratch=[pltpu.VMEM((INPROJ_TM, D_MODEL), BF16)],
        semantics=("parallel", "arbitrary"), vmem_mb=56)


def _gla_level_matrix(c):
    nlv = c.bit_length() - 1
    m = np.zeros(((nlv + 1) * c, c), np.float32)
    for r in range(c):
        m[r, :r + 1] = 1.0
        for p in range(nlv):
            bd = ((r >> (p + 1)) << (p + 1)) + (1 << p) - 1
            if (r >> p) & 1:
                m[(p + 1) * c + r, bd + 1:r + 1] = 1.0
            else:
                m[(p + 1) * c + r, r + 1:bd + 1] = 1.0
    return m


def _hgrn_kernel(*refs, c, nb, layer, has_state):
    if has_state:
        hq_ref, hf_ref, hi_ref, lbp_ref, gn_ref, ms_ref, s0_ref, y_ref, so_ref, s_scr = refs
    else:
        hq_ref, hf_ref, hi_ref, lbp_ref, gn_ref, ms_ref, y_ref, so_ref, s_scr = refs
    nlv = c.bit_length() - 1
    ci = pl.program_id(1)

    @pl.when(ci == 0)
    def _():
        if has_state:
            s_scr[...] = s0_ref[...]
        else:
            s_scr[...] = jnp.zeros_like(s_scr)

    lbp = lbp_ref[...]
    e = jnp.exp(lbp - jnp.max(lbp, axis=0, keepdims=True))
    sm = e / jnp.sum(e, axis=0, keepdims=True)
    lb = jnp.sum(sm[0:layer + 1], axis=0, keepdims=True) - sm[0:1]

    ti = lax.broadcasted_iota(jnp.int32, (c, c), 0)
    si = lax.broadcasted_iota(jnp.int32, (c, c), 1)
    txs = ti ^ si
    lower = ti > si
    ones_c = jnp.ones((c, LANES), F32)

    for ib in range(nb):
        rows = pl.ds(ib * c, c)
        hq, hf, v = hq_ref[rows, :], hf_ref[rows, :], hi_ref[rows, :]
        sig = jax.nn.sigmoid(hf)
        f_gate = lb + (1.0 - lb) * sig
        logf = jnp.log(jnp.maximum(f_gate, F_MIN))
        k = (1.0 - lb) * (1.0 - sig)
        q = jax.nn.silu(hq) * HG_DK ** -0.5
        b_all = _sum_dot(ms_ref[0:c, :], logf)
        lv = _sum_dot(ms_ref[c:, :], logf, terms=2)
        bl_col = _sum_dot_tn(logf, ones_c)
        qk = q * k

        def head(h):
            hs = slice(h * HG_DK, (h + 1) * HG_DK)
            qh, kh, vh, b = q[:, hs], k[:, hs], v[:, hs], b_all[:, hs]
            att = jnp.where(ti == si, jnp.sum(qk[:, hs], axis=-1, keepdims=True), 0.0)
            for p in range(nlv):
                ex = jnp.exp(lv[p * c:(p + 1) * c, hs])
                att = att + jnp.where(((txs >> p) == 1) & lower, _dot_nt(qh * ex, kh * ex), 0.0)
            yield
            s = s_scr[ib, h]
            o = _dot(att, vh) + _dot(qh * jnp.exp(b), s)
            bl = b[c - 1:c]
            s_scr[ib, h] = jnp.exp(bl_col[hs, :]) * s + _dot_tn(kh * jnp.exp(bl - b), vh)
            yield
            y_ref[rows, hs] = _rms(o) * gn_ref[...]

        _round_robin([head(h) for h in range(HG_HEADS)])

    @pl.when(ci == pl.num_programs(1) - 1)
    def _():
        so_ref[...] = s_scr[...]


def _hgrn(z, grp, layer, lower_bounds, gnorm, state, into):
    c = min(CHUNK, grp.seq)
    nb = 1 if grp.seq > c else 2
    nchunk = grp.seq // c
    blk = nb * c
    base = grp.row0 // blk

    def zspec(col0):
        return pl.BlockSpec((blk, BR_W), lambda i, ci: (base + i * nchunk + ci, col0 // BR_W))

    in_specs = [zspec(Z_HG), zspec(Z_HG + BR_W), zspec(Z_HG + 2 * BR_W),
                pl.BlockSpec((DEPTH, BR_W), lambda i, ci: (0, 0)),
                _layer_block(layer, (1, HG_DV), lambda i, ci: (0, 0)),
                pl.BlockSpec(((c.bit_length()) * c, c), lambda i, ci: (0, 0))]
    args = [z, z, z, lower_bounds, gnorm, jnp.asarray(_gla_level_matrix(c))]
    st_spec = _layer_block(layer, (nb, HG_HEADS, HG_DK, HG_DV), lambda i, ci: (i, 0, 0, 0))
    if grp.has_state:
        in_specs.append(st_spec)
        args.append(state)
    return _pcall(
        functools.partial(_hgrn_kernel, c=c, nb=nb, layer=layer, has_state=grp.has_state),
        name="hgrn_" + grp.name, grid=(grp.batch // nb, nchunk), in_specs=in_specs, args=args,
        out_specs=[pl.BlockSpec((blk, BR_W), lambda i, ci: (base + i * nchunk + ci, 0)), st_spec],
        out_shape=[jax.ShapeDtypeStruct((N_ROWS, BR_W), F32),
                   jax.ShapeDtypeStruct((DEPTH, grp.batch, HG_HEADS, HG_DK, HG_DV), F32)],
        into=into, scratch=[pltpu.VMEM((nb, HG_HEADS, HG_DK, HG_DV), F32)],
        semantics=("parallel", "arbitrary"), vmem_mb=48)


def _rope_kernel(inv_ref, cos_ref, sin_ref, *, pos0):
    t = cos_ref.shape[0]
    pos = (pos0 + lax.broadcasted_iota(jnp.int32, (t, RN_DK // 2), 0)).astype(F32)
    ang = pos * inv_ref[...]
    cos_ref[...] = jnp.cos(ang)
    sin_ref[...] = jnp.sin(ang)


def _rope_tables(grp):
    inv = ROPE_BASE ** (-jnp.arange(0, RN_DK, 2, dtype=F32) / RN_DK)
    shp = jax.ShapeDtypeStruct((grp.seq, RN_DK // 2), F32)
    return pl.pallas_call(functools.partial(_rope_kernel, pos0=grp.pos0), out_shape=[shp, shp],
                          name="rope")(inv[None, :])


def _ret_kernel(*refs, c, nb, has_state):
    if has_state:
        (q_ref, k_ref, v_ref, cos_ref, sin_ref, dm_ref, qd_ref, kd_ref, cd_ref, s0_ref,
         y_ref, so_ref, s_scr) = refs
    else:
        (q_ref, k_ref, v_ref, cos_ref, sin_ref, dm_ref, qd_ref, kd_ref, cd_ref,
         y_ref, so_ref, s_scr) = refs
    ci = pl.program_id(1)

    @pl.when(ci == 0)
    def _():
        if has_state:
            s_scr[...] = s0_ref[...]
        else:
            s_scr[...] = jnp.zeros_like(s_scr)

    cos, sin = cos_ref[...], sin_ref[...]
    half = RN_DK // 2

    def rot(x):
        x1, x2 = x[:, :half], x[:, half:]
        return jnp.concatenate([x1 * cos - x2 * sin, x1 * sin + x2 * cos], axis=1)

    def head(ib, h):
        rows, hs = pl.ds(ib * c, c), slice(h * RN_DK, (h + 1) * RN_DK)
        q = rot(q_ref[rows, hs])
        k = rot(k_ref[rows, hs]) * RN_DK ** -0.5
        v = v_ref[rows, hs]
        s = s_scr[ib, h]
        att = _dot_nt(q, k) * dm_ref[h]
        qs = _dot(q, s)
        s_scr[ib, h] = cd_ref[h] * s + _dot_tn(k * kd_ref[h], v)
        yield
        o = _dot(att, v) + qs * qd_ref[h]
        yield
        y_ref[rows, hs] = _rms(o)

    _round_robin([head(ib, h) for ib in range(nb) for h in range(RN_HEADS)])

    @pl.when(ci == pl.num_programs(1) - 1)
    def _():
        so_ref[...] = s_scr[...]


def _retention(z, grp, tables, state, layer, into):
    c = min(CHUNK, grp.seq)
    nb = 1 if grp.seq > c else 2
    nchunk = grp.seq // c
    blk = nb * c
    base = grp.row0 // blk
    log_gamma = jnp.log(1.0 - 2.0 ** (-5.0 - jnp.arange(RN_HEADS, dtype=F32)))
    j = jnp.arange(c, dtype=F32)
    rel = j[:, None] - j[None, :]
    dmat = jnp.where(rel >= 0, jnp.exp(log_gamma[:, None, None] * jnp.maximum(rel, 0.0)), 0.0)
    q_dec = jnp.exp(log_gamma[:, None] * (j + 1.0))[..., None]
    k_dec = jnp.exp(log_gamma[:, None] * (c - 1.0 - j))[..., None]
    c_dec = jnp.exp(log_gamma * c)[:, None, None]

    def zspec(col0):
        return pl.BlockSpec((blk, BR_W), lambda i, ci: (base + i * nchunk + ci, col0 // BR_W))

    def const(shape):
        return pl.BlockSpec(shape, lambda i, ci: (0,) * len(shape))

    tspec = pl.BlockSpec((c, RN_DK // 2), lambda i, ci: (ci, 0))
    in_specs = [zspec(Z_RN), zspec(Z_RN + BR_W), zspec(Z_RN + 2 * BR_W), tspec, tspec,
                const((RN_HEADS, c, c)), const((RN_HEADS, c, 1)), const((RN_HEADS, c, 1)), const((RN_HEADS, 1, 1))]
    args = [z, z, z, tables[0], tables[1], dmat, q_dec, k_dec, c_dec]
    st_spec = _layer_block(layer, (nb, RN_HEADS, RN_DK, RN_DV), lambda i, ci: (i, 0, 0, 0))
    if grp.has_state:
        in_specs.append(st_spec)
        args.append(state)
    return _pcall(
        functools.partial(_ret_kernel, c=c, nb=nb, has_state=grp.has_state),
        name="ret_" + grp.name, grid=(grp.batch // nb, nchunk), in_specs=in_specs, args=args,
        out_specs=[pl.BlockSpec((blk, BR_W), lambda i, ci: (base + i * nchunk + ci, 0)), st_spec],
        out_shape=[jax.ShapeDtypeStruct((N_ROWS, BR_W), F32),
                   jax.ShapeDtypeStruct((DEPTH, grp.batch, RN_HEADS, RN_DK, RN_DV), F32)],
        into=into, scratch=[pltpu.VMEM((nb, RN_HEADS, RN_DK, RN_DV), F32)],
        semantics=("parallel", "arbitrary"), vmem_mb=48)


RW_PREP_ROWS = 256


def _rw_token_mix(ur, ul, w0_ref, w2_ref, a0_ref, a2_ref, kk_ref, ka_ref, vmix):
    r, k, v = ur[:, :BR_W], ur[:, BR_W:2 * BR_W], ur[:, 2 * BR_W:]
    wlo, alo = ul[:, :LORA_PAD], ul[:, LORA_PAD:]
    w = -jax.nn.softplus(-(w0_ref[...] + _dot(jnp.tanh(wlo), w2_ref[...]))) - 0.5
    a = jax.nn.sigmoid(a0_ref[...] + _dot(alo, a2_ref[...]))
    kk = k * kk_ref[...]
    ones = _head_ones(MXU_DIM, RW_N)
    kk2 = kk * kk
    nrm2 = jnp.concatenate(
        [_split_dot(kk2[:, g * MXU_DIM:(g + 1) * MXU_DIM], ones) for g in range(BR_W // MXU_DIM)], axis=1)
    kk = kk / jnp.maximum(jnp.sqrt(nrm2), 1e-12)
    k = k * (1.0 + (a - 1.0) * ka_ref[...])
    if vmix is not None:
        v0_ref, v1_ref, v2_ref, vf_ref = vmix
        mix = jax.nn.sigmoid(v0_ref[...] + _dot(_dot(v, v1_ref[...]), v2_ref[...]))
        v = v + (vf_ref[...] - v) * mix
    return r, -jnp.exp(w), k, v, -kk, kk * a


def _rw_prep_kernel(*refs, tt, nb, has_state, with_vmix):
    refs = list(refs)
    zr_ref, zl_ref = refs[:2]
    pos = 2
    if has_state:
        sr_ref, sl_ref = refs[pos:pos + 2]
        pos += 2
    (mur_ref, mul_ref, w0_ref, w2_ref, a0_ref, a2_ref, kk_ref, ka_ref) = refs[pos:pos + 8]
    pos += 8
    if with_vmix:
        v0_ref, v1_ref, v2_ref, vf_ref = refs[pos:pos + 4]
        pos += 4
    r_ref, d_ref, k_ref, v_ref, a_ref, b_ref, lastr_ref, lastl_ref = refs[pos:pos + 8]
    pr_scr, pl_scr = refs[pos + 8:pos + 10]
    ti = pl.program_id(1)
    rows = nb * tt

    zr, zl = zr_ref[...], zl_ref[...]
    for ib in range(nb):
        lastr_ref[ib] = zr_ref[pl.ds(ib * tt + tt - 1, 1), :]
        lastl_ref[ib] = zl_ref[pl.ds(ib * tt + tt - 1, 1), :]
    first = (lax.broadcasted_iota(jnp.int32, (rows, 1), 0) % tt) == 0

    def prev_rows(z, st_ref, scr):
        w = z.shape[1]
        if nb > 1:
            st = st_ref[...]
            return jnp.broadcast_to(st[:, None, :], (nb, tt, w)).reshape(rows, w)

        @pl.when(ti == 0)
        def _():
            scr[0:1, :] = st_ref[...] if has_state else jnp.zeros((1, w), F32)

        return scr[0:1, :]

    def shifted(z, st_ref, scr, mu):
        prev = jnp.where(first, prev_rows(z, st_ref, scr), pltpu.roll(z, 1, 0))
        scr[0:1, :] = z[rows - 1:rows, :]
        return z + (prev - z) * mu

    ur = shifted(zr, sr_ref if has_state else None, pr_scr, mur_ref[...])
    ul = shifted(zl, sl_ref if has_state else None, pl_scr, mul_ref[...])
    vmix = (v0_ref, v1_ref, v2_ref, vf_ref) if with_vmix else None
    outs = _rw_token_mix(ur, ul, w0_ref, w2_ref, a0_ref, a2_ref, kk_ref, ka_ref, vmix)
    for ref, val in zip((r_ref, d_ref, k_ref, v_ref, a_ref, b_ref), outs):
        ref[...] = val


def _rw_prep(z, grp, prm, state_rkv, state_lora, v_first):
    tt = min(RW_PREP_ROWS, grp.seq)
    nb = RW_PREP_ROWS // tt
    ntile = grp.seq // tt
    rows = nb * tt
    base = grp.row0 // rows
    with_vmix = v_first is not None
    w_rkv, w_lora = 3 * BR_W, 2 * LORA_PAD

    def rowmap(i, t):
        return base + i * ntile + t

    def const(shape):
        return pl.BlockSpec(shape, lambda i, t: (0,) * len(shape))

    in_specs = [pl.BlockSpec((rows, w_rkv), lambda i, t: (rowmap(i, t), Z_RKV // w_rkv)),
                pl.BlockSpec((rows, w_lora), lambda i, t: (rowmap(i, t), Z_LORA // w_lora))]
    args = [z, z]
    if grp.has_state:
        in_specs += [pl.BlockSpec((nb, w_rkv), lambda i, t: (i, 0)),
                     pl.BlockSpec((nb, w_lora), lambda i, t: (i, 0))]
        args += [state_rkv, state_lora]
    in_specs += [const((1, w_rkv)), const((1, w_lora)), const((1, BR_W)), const((LORA_PAD, BR_W)),
                 const((1, BR_W)), const((LORA_PAD, BR_W)), const((1, BR_W)), const((1, BR_W))]
    args += [prm['mu_rkv'], prm['mu_lora'], prm['w0'], prm['w2'], prm['a0'], prm['a2'], prm['k_k'], prm['k_a']]
    out_spec = pl.BlockSpec((rows, BR_W), lambda i, t: (i * ntile + t, 0))
    if with_vmix:
        in_specs += [const((1, BR_W)), const((BR_W, LANES)), const((LANES, BR_W)), out_spec]
        args += [prm['v0'], prm['v1'], prm['v2'], v_first]
    out = jax.ShapeDtypeStruct((grp.rows, BR_W), F32)

    def last_spec(w):
        return pl.BlockSpec((nb, 1, w), lambda i, t: (i, 0, 0))

    def last_shape(w):
        return jax.ShapeDtypeStruct((grp.batch, 1, w), F32)

    return pl.pallas_call(
        functools.partial(_rw_prep_kernel, tt=tt, nb=nb, has_state=grp.has_state, with_vmix=with_vmix),
        grid=(grp.batch // nb, ntile),
        in_specs=in_specs,
        out_specs=[out_spec] * 6 + [last_spec(w_rkv), last_spec(w_lora)],
        out_shape=[out] * 6 + [last_shape(w_rkv), last_shape(w_lora)],
        scratch_shapes=[pltpu.VMEM((SUBLANES, w_rkv), F32), pltpu.VMEM((SUBLANES, w_lora), F32)],
        compiler_params=pltpu.CompilerParams(
            dimension_semantics=("parallel", "arbitrary"), vmem_limit_bytes=48 << 20),
        name="rw_prep_" + ("sample" if grp.has_state else "prompt"),
    )(*args)


RW_TILE = MXU_DIM
RW_TILE_HEADS = RW_TILE // RW_N
RW_LANE_HEADS = LANES // RW_N
RW_SCAN_UNROLL = 4


def _rw_scan_kernel(r_ref, d_ref, k_ref, v_ref, a_ref, b_ref, lw_ref, lb_ref, rk_ref, s0_ref,
                    y_ref, so_ref, s_scr, v_scr, y_scr):
    t = pl.program_id(1)

    @pl.when(t == 0)
    def _():
        s_scr[...] = s0_ref[...]

    r, ld, k, a, b = (ref[...].T for ref in (r_ref, d_ref, k_ref, a_ref, b_ref))
    v_scr[...] = v_ref[...].T
    w = jnp.exp(ld)

    for h in range(RW_LANE_HEADS):
        hs = slice(h * RW_N, (h + 1) * RW_N)
        rh, wh, kh, ah, bh = r[hs], w[hs], k[hs], a[hs], b[hs]

        def row_update(i, carry, h=h, rh=rh, wh=wh, kh=kh, ah=ah, bh=bh):
            s = s_scr[h, i]
            sa = jnp.sum(s * ah, axis=0, keepdims=True)
            s = s * wh + sa * bh + v_scr[pl.ds(h * RW_N + i, 1), :] * kh
            s_scr[h, i] = s
            y_scr[pl.ds(h * RW_N + i, 1), :] = jnp.sum(s * rh, axis=0, keepdims=True)
            return carry

        lax.fori_loop(0, RW_N, row_update, 0, unroll=RW_SCAN_UNROLL)

        y, vh = y_scr[hs, :], v_scr[hs, :]
        mu = jnp.mean(y, axis=0, keepdims=True)
        dlt = y - mu
        var = jnp.mean(dlt * dlt, axis=0, keepdims=True)
        yn = dlt * lax.rsqrt(var + RW_GN_EPS) * lw_ref[h] + lb_ref[h]
        bonus = jnp.sum(rh * kh * rk_ref[h], axis=0, keepdims=True)
        y_scr[hs, :] = yn + bonus * vh

    y_ref[...] = y_scr[...].T

    @pl.when(t == pl.num_programs(1) - 1)
    def _():
        so_ref[...] = s_scr[...]


def _rw_scan(seqs, grp, norm, state_t, layer, into):
    assert grp.has_state and grp.batch == LANES
    hp = RW_LANE_HEADS
    width = hp * RW_N
    tm = jnp.stack(seqs).reshape(len(seqs), grp.batch, grp.seq, BR_W).transpose(0, 2, 1, 3)
    seq_specs = [pl.BlockSpec((None, None, grp.batch, width), lambda j, t, n=n: (n, t, 0, j))
                 for n in range(len(seqs))]
    par_spec = pl.BlockSpec((hp, RW_N, 1), lambda j, t: (j, 0, 0))
    st_spec = _layer_block(layer, (hp, RW_N, RW_N, grp.batch), lambda j, t: (j, 0, 0, 0))
    y, st = _pcall(
        _rw_scan_kernel, name="rw_scan_" + grp.name, grid=(RW_HEADS // hp, grp.seq),
        in_specs=seq_specs + [par_spec] * 3 + [st_spec],
        args=[tm] * len(seqs) + [p.reshape(RW_HEADS, RW_N, 1) for p in norm] + [state_t],
        out_specs=[pl.BlockSpec((None, grp.batch, width), lambda j, t: (t, 0, j)), st_spec],
        out_shape=[jax.ShapeDtypeStruct((grp.seq, grp.batch, BR_W), F32),
                   jax.ShapeDtypeStruct((DEPTH, RW_HEADS, RW_N, RW_N, grp.batch), F32)],
        into=(None, into[1]),
        scratch=[pltpu.VMEM((hp, RW_N, RW_N, grp.batch), F32), pltpu.VMEM((width, grp.batch), F32),
                 pltpu.VMEM((width, grp.batch), F32)],
        semantics=("parallel", "arbitrary"), vmem_mb=48)
    y_rows = y.transpose(1, 0, 2).reshape(grp.rows, BR_W)
    return lax.dynamic_update_slice(into[0], y_rows, (grp.row0, 0)), st


RW_CHUNK = RW_N
RW_CHUNKS_PER_STEP = 2
RW_SOLVE_BASE = 8


def _rw_chunk_kernel(*refs, with_vmix):
    refs = list(refs)
    zr_ref, zl_ref, mur_ref, mul_ref, w0_ref, w2_ref, a0_ref, a2_ref, kk_ref, ka_ref = refs[:10]
    pos = 10
    vmix = None
    if with_vmix:
        vmix = tuple(refs[pos:pos + 4])
        pos += 4
    lw_ref, lb_ref, rk_ref, y_ref, ho_ref, vout_ref, lastr_ref, lastl_ref = refs[pos:pos + 8]
    h_scr, seq_scr, pr_scr, pl_scr = refs[pos + 8:pos + 12]
    ci = pl.program_id(1)
    ln, tw, nt = RW_CHUNK, RW_TILE, BR_W // RW_TILE
    nrows = ln * RW_CHUNKS_PER_STEP

    @pl.when(ci == 0)
    def _():
        h_scr[...] = jnp.zeros_like(h_scr)
        pr_scr[...] = jnp.zeros_like(pr_scr)
        pl_scr[...] = jnp.zeros_like(pl_scr)

    first = lax.broadcasted_iota(jnp.int32, (nrows, 1), 0) == 0

    def shifted(z_ref, scr, mu, last_ref):
        z = z_ref[...]
        prev = jnp.where(first, scr[0:1, :], pltpu.roll(z, 1, 0))
        scr[0:1, :] = z[nrows - 1:nrows, :]
        last_ref[0] = z[nrows - 1:nrows, :]
        return z + (prev - z) * mu

    ur = shifted(zr_ref, pr_scr, mur_ref[...], lastr_ref)
    ul = shifted(zl_ref, pl_scr, mul_ref[...], lastl_ref)
    for n, val in enumerate(_rw_token_mix(ur, ul, w0_ref, w2_ref, a0_ref, a2_ref, kk_ref, ka_ref, vmix)):
        seq_scr[n] = val
    vout_ref[...] = seq_scr[3]

    row = lax.broadcasted_iota(jnp.int32, (tw, tw), 0)
    col = lax.broadcasted_iota(jnp.int32, (tw, tw), 1)
    same_head = (row // RW_N) == (col // RW_N)
    ones = same_head.astype(BF16)
    t_i = lax.broadcasted_iota(jnp.int32, (ln, tw), 0)
    s_i = lax.broadcasted_iota(jnp.int32, (ln, tw), 1) % ln
    strict, incl = s_i < t_i, s_i <= t_i
    tril = (lax.broadcasted_iota(jnp.int32, (ln, ln), 0) >= lax.broadcasted_iota(jnp.int32, (ln, ln), 1)).astype(F32)

    def blocks(x):
        return jnp.where(same_head, jnp.concatenate([x] * RW_TILE_HEADS, axis=0), 0.0).astype(BF16)

    def mm(x, w):
        return jnp.dot(_bf(x), w, preferred_element_type=F32)

    def prepare(j, g, out):
        rows, sl = pl.ds(j * ln, ln), slice(g * tw, (g + 1) * tw)
        r, ld, k, v, a, b = (seq_scr[n, rows, sl] for n in range(6))
        c = _sum_dot(tril, ld)
        tot_col = _sum_dot_tn(ld, jnp.ones((ln, LANES), F32))
        yield
        c_last = c[ln - 1:ln]
        e_inv, e_end = jnp.exp(-c), jnp.exp(c_last - c)
        at, rt = a * jnp.exp(c - ld), r * jnp.exp(c)
        bt, kt, bp, kp = b * e_inv, k * e_inv, b * e_end, k * e_end
        gram = lax.dot_general(_bf(jnp.concatenate([at, rt], axis=0)),
                               jnp.concatenate([blocks(bt), blocks(kt)], axis=0),
                               (((1,), (1,)), ((), ())), preferred_element_type=F32)
        yield
        n_ab = jnp.where(strict, gram[:ln, :tw], 0.0)
        a_ak = jnp.where(strict, gram[:ln, tw:], 0.0)
        a_rb = jnp.where(incl, gram[ln:, :tw], 0.0)
        a_rk = jnp.where(incl, gram[ln:, tw:], 0.0)
        av = mm(jnp.concatenate([a_ak, a_rk], axis=0), blocks(v))
        yield
        def same_block(m):
            return (t_i // m) == (s_i // m)

        nd = jnp.where(same_block(RW_SOLVE_BASE), n_ab, 0.0)
        tinv = jnp.where(s_i == t_i, 1.0, 0.0) + nd
        p = mm(nd, blocks(nd))
        yield
        m = 2
        while m < RW_SOLVE_BASE:
            m *= 2
            if m < RW_SOLVE_BASE:
                both = mm(jnp.concatenate([tinv, p], axis=0), blocks(p))
                tinv, p = tinv + both[:ln], both[ln:]
            else:
                tinv = tinv + mm(tinv, blocks(p))
            yield
        m = RW_SOLVE_BASE
        while m < ln:
            lower_left = jnp.where(same_block(2 * m) & jnp.logical_not(same_block(m)), n_ab, 0.0)
            tn = mm(tinv, blocks(lower_left))
            yield
            tinv = tinv + mm(tn, blocks(tinv))
            yield
            m *= 2
        w = mm(tinv, jnp.concatenate([blocks(at), blocks(av[:ln])], axis=1))
        yield
        at2, uv = w[:, :tw], w[:, tw:]
        m_off = jnp.where(same_head, _dot_tn(bp, at2), 0.0)
        cc = jnp.where(same_head, _dot_tn(jnp.concatenate([bp, kp], axis=0), jnp.concatenate([uv, v], axis=0)), 0.0)
        qy = mm(a_rb, jnp.concatenate([blocks(at2), blocks(uv)], axis=1))
        decay_col = jnp.concatenate([jnp.exp(tot_col)] * (tw // LANES), axis=1)
        out[j, g] = (rt + qy[:, :tw], qy[:, tw:] + av[ln:], m_off, cc, decay_col)

    items = [(j, g) for j in range(RW_CHUNKS_PER_STEP) for g in range(nt)]
    prepared = {}
    _round_robin([prepare(j, g, prepared) for j, g in items])

    for j, g in items:
        q, yc, m_off, cc, decay_col = prepared[(j, g)]
        rows, sl = pl.ds(j * ln, ln), slice(g * tw, (g + 1) * tw)
        h = h_scr[g]
        qm = mm(jnp.concatenate([q, m_off], axis=0), _bf(h))
        y = qm[:ln] + yc
        h_scr[g] = decay_col * h + qm[ln:] + cc

        mu = _split_dot(y, ones) * (1.0 / RW_N)
        dlt = y - mu
        var = _split_dot(dlt * dlt, ones) * (1.0 / RW_N)
        yn = dlt * lax.rsqrt(var + RW_GN_EPS) * lw_ref[:, sl] + lb_ref[:, sl]
        bonus = _split_dot(seq_scr[0, rows, sl] * seq_scr[2, rows, sl] * rk_ref[:, sl], ones)
        y_ref[rows, sl] = yn + bonus * seq_scr[3, rows, sl]

    @pl.when(ci == pl.num_programs(1) - 1)
    def _():
        for g in range(nt):
            h = h_scr[g]
            for hd in range(RW_TILE_HEADS):
                ho_ref[0, g * RW_TILE_HEADS + hd] = h[hd * RW_N:(hd + 1) * RW_N, hd * RW_N:(hd + 1) * RW_N]


def _rw_scan_chunked(z, grp, prm, norm, v_first, layer, into):
    blk = RW_CHUNK * RW_CHUNKS_PER_STEP
    assert not grp.has_state and grp.seq % blk == 0 and grp.row0 % blk == 0
    nstep = grp.seq // blk
    base = grp.row0 // blk
    with_vmix = v_first is not None
    w_rkv, w_lora = 3 * BR_W, 2 * LORA_PAD

    def const(shape):
        return pl.BlockSpec(shape, lambda i, c: (0,) * len(shape))

    seq_spec = pl.BlockSpec((blk, BR_W), lambda i, c: (i * nstep + c, 0))
    in_specs = [pl.BlockSpec((blk, w_rkv), lambda i, c: (base + i * nstep + c, Z_RKV // w_rkv)),
                pl.BlockSpec((blk, w_lora), lambda i, c: (base + i * nstep + c, Z_LORA // w_lora)),
                const((1, w_rkv)), const((1, w_lora)), const((1, BR_W)), const((LORA_PAD, BR_W)),
                const((1, BR_W)), const((LORA_PAD, BR_W)), const((1, BR_W)), const((1, BR_W))]
    args = [z, z, prm['mu_rkv'], prm['mu_lora'], prm['w0'], prm['w2'], prm['a0'], prm['a2'], prm['k_k'], prm['k_a']]
    if with_vmix:
        in_specs += [const((1, BR_W)), const((BR_W, LANES)), const((LANES, BR_W)), seq_spec]
        args += [prm['v0'], prm['v1'], prm['v2'], v_first]
    in_specs += [const((1, BR_W))] * 3
    args += list(norm)

    def last_spec(w):
        return pl.BlockSpec((1, 1, w), lambda i, c: (i, 0, 0))

    return _pcall(
        functools.partial(_rw_chunk_kernel, with_vmix=with_vmix),
        name="rw_chunk_" + grp.name, grid=(grp.batch, nstep), in_specs=in_specs, args=args,
        out_specs=[pl.BlockSpec((blk, BR_W), lambda i, c: (base + i * nstep + c, 0)),
                   _layer_block(layer, (1, RW_HEADS, RW_N, RW_N), lambda i, c: (i, 0, 0, 0)),
                   seq_spec, last_spec(w_rkv), last_spec(w_lora)],
        out_shape=[jax.ShapeDtypeStruct((N_ROWS, BR_W), F32),
                   jax.ShapeDtypeStruct((DEPTH, grp.batch, RW_HEADS, RW_N, RW_N), F32),
                   jax.ShapeDtypeStruct((grp.rows, BR_W), F32),
                   jax.ShapeDtypeStruct((grp.batch, 1, w_rkv), F32),
                   jax.ShapeDtypeStruct((grp.batch, 1, w_lora), F32)],
        into=tuple(into) + (None, None, None),
        scratch=[pltpu.VMEM((BR_W // RW_TILE, RW_TILE, RW_TILE), F32), pltpu.VMEM((6, blk, BR_W), F32),
                 pltpu.VMEM((SUBLANES, w_rkv), F32), pltpu.VMEM((SUBLANES, w_lora), F32)],
        semantics=("parallel", "arbitrary"), vmem_mb=48)


MERGE_TM = 256


def _merge_kernel(y0_ref, y1_ref, y2_ref, g_ref, zm_ref, wb_ref, m_ref):
    acc = None
    for n, y_ref in enumerate((y0_ref, y1_ref, y2_ref)):
        ys = y_ref[...] * jax.nn.silu(g_ref[:, n * BR_W:(n + 1) * BR_W])
        proj = jnp.dot(_bf(ys), wb_ref[n], preferred_element_type=F32)
        term = jax.nn.sigmoid(zm_ref[:, n * D_MODEL:(n + 1) * D_MODEL]) * proj
        acc = term if acc is None else acc + term
    m_ref[...] = _bf(acc)


def _merge(ys, z, wb, layer):
    n = z.shape[0]
    yspec = pl.BlockSpec((MERGE_TM, BR_W), lambda i: (i, 0))
    gw, mw = N_BRANCH * BR_W, N_BRANCH * D_MODEL
    return _pcall(
        _merge_kernel, name="merge", grid=(n // MERGE_TM,),
        in_specs=[yspec, yspec, yspec,
                  pl.BlockSpec((MERGE_TM, gw), lambda i: (i, Z_GATE // gw)),
                  pl.BlockSpec((MERGE_TM, mw), lambda i: (i, Z_MERGE // mw)),
                  pl.BlockSpec((None, N_BRANCH, BR_W, D_MODEL), lambda i: (layer, 0, 0, 0),
                               pipeline_mode=pl.Buffered(1))],
        args=[*ys, z, z, wb],
        out_specs=pl.BlockSpec((MERGE_TM, D_MODEL), lambda i: (i, 0)),
        out_shape=jax.ShapeDtypeStruct((n, D_MODEL), BF16),
        semantics=("parallel",), vmem_mb=48)


POST_TM = 256


def _post_kernel(xp_ref, xs_ref, m_ref, pp_ref, ps_ref, wo_ref, np_ref, pu_ref, pn_ref, pg_ref, op_ref, os_ref,
                 *, npt):
    is_prompt = pl.program_id(0) < npt
    x = jnp.where(is_prompt, xp_ref[...], xs_ref[...])
    p = jnp.where(is_prompt, pp_ref[...], ps_ref[...])
    x = x + _rms(jnp.dot(m_ref[...], wo_ref[...], preferred_element_type=F32)) * np_ref[...]
    e = _rms(jnp.dot(_bf(p), pu_ref[...], preferred_element_type=F32)) * pn_ref[...]
    out = x + e * jax.nn.sigmoid(jnp.dot(_bf(x), pg_ref[...], preferred_element_type=F32))

    @pl.when(is_prompt)
    def _():
        op_ref[...] = out

    @pl.when(jnp.logical_not(is_prompt))
    def _():
        os_ref[...] = out


def _post(x_pair, m, p_pair, wo, norm_post, ple_up, ple_norm, ple_gate, layer):
    pmap, smap, npt = _row_pair_maps(POST_TM)

    def const(shape):
        return pl.BlockSpec((None,) + shape, lambda i: (layer, 0, 0), pipeline_mode=pl.Buffered(1))

    def pair(w):
        return [pl.BlockSpec((POST_TM, w), pmap), pl.BlockSpec((POST_TM, w), smap)]

    ple_pair = [_layer_block(layer, (POST_TM, PLE_DIM), pmap), _layer_block(layer, (POST_TM, PLE_DIM), smap)]
    return _pcall(
        functools.partial(_post_kernel, npt=npt), name="post", grid=(N_ROWS // POST_TM,),
        in_specs=pair(D_MODEL) + [pl.BlockSpec((POST_TM, D_MODEL), lambda i: (i, 0))] + ple_pair
        + [const((D_MODEL, D_MODEL)), const((1, D_MODEL)), const((PLE_DIM, D_MODEL)),
           const((1, D_MODEL)), const((D_MODEL, D_MODEL))],
        args=[*x_pair, m, *p_pair, wo, norm_post, ple_up, ple_norm, ple_gate],
        out_specs=pair(D_MODEL),
        out_shape=[jax.ShapeDtypeStruct((N_PROMPT, D_MODEL), F32), jax.ShapeDtypeStruct((N_SAMPLE, D_MODEL), F32)],
        semantics=("arbitrary",), vmem_mb=48)


def _pad_rows(a, rows):
    return jnp.concatenate([a, jnp.zeros((rows - a.shape[0],) + a.shape[1:], a.dtype)], axis=0)


def _pad_cols(a, cols):
    return jnp.concatenate([a, jnp.zeros(a.shape[:-1] + (cols - a.shape[-1],), a.dtype)], axis=-1)


def _layer_rows(a):
    return a.reshape(a.shape[0], 1, a.shape[1])


def kernel(x_prompt, x_sample, state_hgrn, state_rwkv, state_shift, state_ret, p_prompt, p_sample,
           norm_pre, w_in, hg_lower_bounds, hg_norm, rw_mu, rw_w0, rw_w2, rw_a0, rw_a2, rw_k_k, rw_k_a,
           rw_v0, rw_v1, rw_v2, rw_r_k, rw_ln_w, rw_ln_b, w_branch, w_out, norm_post, ple_up, ple_norm, ple_gate):
    groups = (PROMPT, SAMPLE)
    x = (x_prompt.reshape(N_PROMPT, D_MODEL), x_sample.reshape(N_SAMPLE, D_MODEL))
    p = (p_prompt.reshape(DEPTH, N_PROMPT, PLE_DIM), p_sample.reshape(DEPTH, N_SAMPLE, PLE_DIM))
    w_packed = _pack_w_in(w_in)
    wb, wo, pu, pg = _bf(w_branch), _bf(w_out), _bf(ple_up), _bf(ple_gate)
    g_pre, g_hg, g_post, g_ple = (_layer_rows(a) for a in (norm_pre, hg_norm, norm_post, ple_norm))
    tables = [_rope_tables(g) for g in groups]
    state_rwkv_t = jnp.transpose(state_rwkv, (0, 2, 3, 4, 1))
    v_first = [None, None]
    shifts = [[], []]
    st_out = [[None, None, None] for _ in groups]

    for l in range(DEPTH):
        z = _inproj(x, g_pre, w_packed, l)
        mu_rkv, mu_lora = _pack_shift_cols(rw_mu[l][None])
        prm = dict(mu_rkv=mu_rkv, mu_lora=mu_lora, w0=rw_w0[l][None],
                   w2=_bf(_pad_rows(rw_w2[l], LORA_PAD)), a0=rw_a0[l][None],
                   a2=_bf(_pad_rows(rw_a2[l], LORA_PAD)), k_k=rw_k_k[l][None], k_a=rw_k_a[l][None])
        if l > 0:
            prm.update(v0=rw_v0[l - 1][None], v1=_bf(_pad_cols(rw_v1[l - 1], LANES)),
                       v2=_bf(_pad_rows(rw_v2[l - 1], LANES)))
        rw_norm = (rw_ln_w[l][None], rw_ln_b[l][None], rw_r_k[l].reshape(1, BR_W))
        ys = [None, None, None]
        for gi, grp in enumerate(groups):
            st_rkv, st_lora = _pack_shift_cols(state_shift[l]) if grp.has_state else (None, None)
            ys[0], st_out[gi][0] = _hgrn(z, grp, l, hg_lower_bounds, g_hg, state_hgrn, (ys[0], st_out[gi][0]))
            if grp.has_state:
                *seqs, last_rkv, last_lora = _rw_prep(z, grp, prm, st_rkv, st_lora, v_first[gi])
                ys[1], st_out[gi][1] = _rw_scan(seqs, grp, rw_norm, state_rwkv_t, l, (ys[1], st_out[gi][1]))
                v_rows = seqs[3]
            else:
                ys[1], st_out[gi][1], v_rows, last_rkv, last_lora = _rw_scan_chunked(
                    z, grp, prm, rw_norm, v_first[gi], l, (ys[1], st_out[gi][1]))
            shifts[gi].append(_unpack_shift_cols(last_rkv[:, 0], last_lora[:, 0]))
            if l == 0:
                v_first[gi] = v_rows
            ys[2], st_out[gi][2] = _retention(z, grp, tables[gi], state_ret, l, (ys[2], st_out[gi][2]))
        m = _merge(ys, z, wb, l)
        x = _post(x, m, p, wo, g_post, pu, g_ple, pg, l)

    y_prompt = x[0].reshape(BATCH, SEQ, D_MODEL)
    y_sample = x[1].reshape(DEC_BATCH, DEC_SEQ, D_MODEL)
    (hg_p, rw_p, rn_p), (hg_s, rw_s, rn_s) = st_out
    rw_p = jnp.swapaxes(rw_p, -1, -2)
    rw_s = jnp.transpose(rw_s, (0, 4, 1, 2, 3))
    sh_p, sh_s = (jnp.stack(s, axis=0) for s in shifts)
    return (y_prompt, y_sample, hg_p, hg_s, rw_p, rw_s, sh_p, sh_s, rn_p, rn_s)
```
